```python
import math
import jax, jax.numpy as jnp
from jax import lax
import numpy as np

D_MODEL = 4096
BATCH = 2
SEQ = 8192
DEPTH = 4
DEC_BATCH = 8
DEC_SEQ = 2048
PAST_LEN = 128

N_MIXERS = 3
N_HYENA = (DEPTH + 2) // 3
N_RET = (DEPTH + 1) // 3
N_GLA = DEPTH // 3
D_FF = 4 * D_MODEL
EPS = 1e-6
HY_FILTER_WIDTH = 64
HY_BANDS = 16
HY_EMB = 2 * HY_BANDS + 1
HY_FAST_DECAY = 0.3
HY_SLOW_DECAY = 1.5
HY_TARGET = 1e-2
HY_SHIFT = 0.05
RET_HEADS = 16
RET_HEAD_DIM = D_MODEL // RET_HEADS
RET_CHUNK = 128
ROPE_BASE = 10000.0
GLA_HEADS = 4
GLA_KEY_DIM = D_MODEL // 2
GLA_DK = GLA_KEY_DIM // GLA_HEADS
GLA_DV = D_MODEL // GLA_HEADS
GLA_GATE_RANK = 16
GLA_TAU = 16.0
GLA_CHUNK = 64

kernel_name = "hybrid_bidir_hyena_retnet_gla_encoder"

F32 = jnp.float32


def _rmsnorm(x, g):
    xf = x.astype(F32)
    y = xf * lax.rsqrt(jnp.mean(xf * xf, axis=-1, keepdims=True) + EPS)
    return (y * g.astype(F32)).astype(x.dtype)


def _head_rmsnorm(o):
    return o * lax.rsqrt(jnp.mean(o * o, axis=-1, keepdims=True) + EPS)


def _flip(t):
    return t[:, :, ::-1]


def _to_chunks(t, c):
    b, h, l, d = t.shape
    return jnp.moveaxis(t.reshape(b, h, l // c, c, d), 2, 0)


def _from_chunks(t):
    n, b, h, c, d = t.shape
    return jnp.moveaxis(t, 0, 2).reshape(b, h, n * c, d)


def _short_conv(z, w, b):
    zp = jnp.pad(z, ((0, 0), (1, 1), (0, 0)))
    return zp[:, :-2] * w[0] + zp[:, 1:-1] * w[1] + zp[:, 2:] * w[2] + b


def _hyena_filter(L, w1, b1, w2, b2, w3, b3, w_out, freq):
    t = jnp.linspace(0.0, 1.0, L, dtype=F32)[:, None]
    w = (2.0 * math.pi / L) * jnp.arange(L, dtype=F32)[:, None]
    bands = jnp.linspace(1e-4, HY_BANDS - 1, HY_BANDS, dtype=F32)[None, :]
    z = jnp.concatenate([t, jnp.cos(bands * w), -jnp.sin(bands * w)], axis=-1)
    fr = freq.astype(F32)
    h = jnp.sin(fr * (z @ w1.astype(F32) + b1.astype(F32)))
    h = jnp.sin(fr * (h @ w2.astype(F32) + b2.astype(F32)))
    h = jnp.sin(fr * (h @ w3.astype(F32) + b3.astype(F32)))
    h = h @ w_out.astype(F32)
    max_decay = math.log(HY_TARGET) / HY_FAST_DECAY
    min_decay = math.log(HY_TARGET) / HY_SLOW_DECAY
    deltas = jnp.linspace(min_decay, max_decay, D_MODEL, dtype=F32)
    window = jnp.exp(-t * jnp.abs(deltas)[None, :]) + HY_SHIFT
    h_fwd = h[:, :D_MODEL] * window
    h_bwd = h[:, D_MODEL:] * window
    k = jnp.concatenate([h_fwd, jnp.zeros((1, D_MODEL), F32), h_bwd[:L - 1][::-1]], axis=0)
    return k / (jnp.sum(jnp.abs(k), axis=0, keepdims=True) + EPS)


def _hyena_mixer(u, w_in, b_in, conv_w, conv_b, f_w1, f_b1, f_w2, f_b2, f_w3, f_b3,
                 f_wout, freq, skip, w_out, b_out):
    B, L, _ = u.shape
    z = _short_conv(u @ w_in + b_in, conv_w, conv_b)
    x0, x1, v = jnp.split(z, 3, axis=-1)
    v = (v * x1).astype(F32)
    k = _hyena_filter(L, f_w1, f_b1, f_w2, f_b2, f_w3, f_b3, f_wout, freq)
    vf = jnp.fft.rfft(v, n=2 * L, axis=1)
    kf = jnp.fft.rfft(k, axis=0)
    y = jnp.fft.irfft(vf * kf[None], n=2 * L, axis=1)[:, :L]
    y = y + skip.astype(F32) * v
    y = y.astype(u.dtype) * x0
    return y @ w_out + b_out


def _rotary(x, pos):
    d = x.shape[-1]
    inv = 1.0 / (ROPE_BASE ** jnp.linspace(0.0, 1.0, d // 2, dtype=F32))
    ang = pos[:, None] * inv[None, :]
    cos = jnp.cos(ang)[None, :, None, :]
    sin = jnp.sin(ang)[None, :, None, :]
    xf = x.astype(F32)
    x1, x2 = xf[..., 0::2], xf[..., 1::2]
    return jnp.stack([x1 * cos - x2 * sin, x1 * sin + x2 * cos], axis=-1).reshape(x.shape)


def _retention_chunked(q, k, v, log_gamma, strict):
    B, H, L, dk = q.shape
    dv = v.shape[-1]
    C = RET_CHUNK
    idx = jnp.arange(C, dtype=F32)
    diff = idx[:, None] - idx[None, :]
    mask = (diff > 0) if strict else (diff >= 0)
    lg = log_gamma[:, None, None]
    intra = jnp.where(mask[None], jnp.exp(lg * jnp.where(mask, diff, 0.0)[None]), 0.0)
    q_decay = jnp.exp(log_gamma[:, None] * (idx + 1.0))[None, :, :, None]
    k_decay = jnp.exp(log_gamma[:, None] * (C - 1.0 - idx))[None, :, :, None]
    chunk_decay = jnp.exp(log_gamma * C)[None, :, None, None]

    def step(state, inp):
        qc, kc, vc = inp
        s = jnp.einsum('bhid,bhjd->bhij', qc, kc) * intra[None]
        o = jnp.einsum('bhij,bhjv->bhiv', s, vc) + jnp.einsum('bhid,bhdv->bhiv', qc * q_decay, state)
        state = state * chunk_decay + jnp.einsum('bhjd,bhjv->bhdv', kc * k_decay, vc)
        return state, o

    state0 = jnp.zeros((B, H, dk, dv), F32)
    _, o = lax.scan(step, state0, (_to_chunks(q, C), _to_chunks(k, C), _to_chunks(v, C)))
    return _from_chunks(o)


def _retention_mixer(u, w_in, w_out):
    B, L, _ = u.shape
    q, k, v, g = jnp.split(u @ w_in, 4, axis=-1)
    pos = jnp.arange(L, dtype=F32)
    heads = lambda t: t.reshape(B, L, RET_HEADS, RET_HEAD_DIM)
    q = _rotary(heads(q), pos)
    k = _rotary(heads(k), pos) * (RET_HEAD_DIM ** -0.5)
    v = heads(v).astype(F32)
    q, k, v = (jnp.transpose(t, (0, 2, 1, 3)) for t in (q, k, v))
    hidx = jnp.arange(RET_HEADS, dtype=F32)
    lg_fwd = jnp.log1p(-jnp.exp2(-5.0 - hidx))
    lg_bwd = jnp.log1p(-jnp.exp2(-5.0 - hidx[::-1]))
    o = (_retention_chunked(q, k, v, lg_fwd, False)
         + _flip(_retention_chunked(_flip(q), _flip(k), _flip(v), lg_bwd, True)))
    o = _head_rmsnorm(o)
    o = jnp.transpose(o, (0, 2, 1, 3)).reshape(B, L, D_MODEL).astype(u.dtype)
    return (jax.nn.silu(g) * o) @ w_out


def _gla_chunked(q, k, v, log_a, strict):
    B, H, L, dk = q.shape
    dv = v.shape[-1]
    C = GLA_CHUNK
    idx = jnp.arange(C)
    mask = (idx[:, None] > idx[None, :]) if strict else (idx[:, None] >= idx[None, :])
    m5 = mask[None, None, :, :, None]

    def step(state, inp):
        qc, kc, vc, gc = inp
        b = jnp.cumsum(gc, axis=2)
        rel = b[:, :, :, None, :] - b[:, :, None, :, :]
        decay = jnp.where(m5, jnp.exp(jnp.where(m5, rel, 0.0)), 0.0)
        s = jnp.einsum('bhid,bhjd,bhijd->bhij', qc, kc, decay)
        o = jnp.einsum('bhij,bhjv->bhiv', s, vc) + jnp.einsum('bhid,bhdv->bhiv', qc * jnp.exp(b), state)
        b_last = b[:, :, -1:, :]
        state = (state * jnp.exp(b_last)[:, :, 0, :, None]
                 + jnp.einsum('bhjd,bhjv->bhdv', kc * jnp.exp(b_last - b), vc))
        return state, o

    state0 = jnp.zeros((B, H, dk, dv), F32)
    _, o = lax.scan(step, state0, (_to_chunks(q, C), _to_chunks(k, C), _to_chunks(v, C),
                                   _to_chunks(log_a, C)))
    return _from_chunks(o)


def _gla_mixer(u, w_in, gate_w1, gate_w2, gate_b, w_out):
    B, L, _ = u.shape
    q, k, v, r = jnp.split(u @ w_in, [GLA_KEY_DIM, 2 * GLA_KEY_DIM, 2 * GLA_KEY_DIM + D_MODEL], axis=-1)

    def heads(t, d):
        return jnp.transpose(t.reshape(B, L, GLA_HEADS, d), (0, 2, 1, 3)).astype(F32)

    def log_gate(direction):
        zg = (u @ gate_w1[direction]) @ gate_w2[direction] + gate_b[direction]
        return heads(jax.nn.log_sigmoid(zg.astype(F32)) / GLA_TAU, GLA_DK)

    q = heads(q, GLA_DK) * (GLA_DK ** -0.5)
    k = heads(k, GLA_DK)
    v = heads(v, GLA_DV)
    o = (_gla_chunked(q, k, v, log_gate(0), False)
         + _flip(_gla_chunked(_flip(q), _flip(k), _flip(v), _flip(log_gate(1)), True)))
    o = _head_rmsnorm(o)
    o = jnp.transpose(o, (0, 2, 1, 3)).reshape(B, L, D_MODEL).astype(u.dtype)
    return (jax.nn.silu(r) * o) @ w_out


def _sq_relu_mlp(h, w1, w2):
    a = jax.nn.relu(h @ w1)
    return (a * a) @ w2


def _trunk(x, c, hy, ret, gla, norm_g, ada_w, ada_b, mlp_w1, mlp_w2, final_g):
    cs = jax.nn.silu(c)
    for i in range(DEPTH):
        mod = cs @ ada_w[i] + ada_b[i]
        sh1, sc1, g1, sh2, sc2, g2 = (m[:, None, :] for m in jnp.split(mod, 6, axis=-1))
        h = _rmsnorm(x, norm_g[i, 0]) * (1.0 + sc1) + sh1
        kind, slot = i % N_MIXERS, i // N_MIXERS
        if kind == 0:
            y = _hyena_mixer(h, *[p[slot] for p in hy])
        elif kind == 1:
            y = _retention_mixer(h, *[p[slot] for p in ret])
        else:
            y = _gla_mixer(h, *[p[slot] for p in gla])
        x = x + g1 * y
        h = _rmsnorm(x, norm_g[i, 1]) * (1.0 + sc2) + sh2
        x = x + g2 * _sq_relu_mlp(h, mlp_w1[i], mlp_w2[i])
    return _rmsnorm(x, final_g)


def setup_inputs(seed: int = 0) -> dict:
    key = jax.random.key(seed)
    ks = iter(jax.random.split(key, 40))

    def nrm(shape, scale):
        return jax.random.normal(next(ks), shape, F32) * scale

    D = D_MODEL
    return {
        "x_prompt": nrm((BATCH, SEQ, D), 1.0),
        "x_sample": nrm((DEC_BATCH, DEC_SEQ, D), 1.0),
        "c_prompt": nrm((BATCH, D), 1.0),
        "c_sample": nrm((DEC_BATCH, D), 1.0),
        "hy_w_in": nrm((N_HYENA, D, 3 * D), D ** -0.5),
        "hy_b_in": nrm((N_HYENA, 3 * D), 0.02),
        "hy_conv_w": nrm((N_HYENA, 3, 3 * D), 3 ** -0.5),
        "hy_conv_b": nrm((N_HYENA, 3 * D), 0.02),
        "hy_f_w1": nrm((N_HYENA, HY_EMB, HY_FILTER_WIDTH), HY_EMB ** -0.5),
        "hy_f_b1": nrm((N_HYENA, HY_FILTER_WIDTH), 0.02),
        "hy_f_w2": nrm((N_HYENA, HY_FILTER_WIDTH, HY_FILTER_WIDTH), HY_FILTER_WIDTH ** -0.5),
        "hy_f_b2": nrm((N_HYENA, HY_FILTER_WIDTH), 0.02),
        "hy_f_w3": nrm((N_HYENA, HY_FILTER_WIDTH, HY_FILTER_WIDTH), HY_FILTER_WIDTH ** -0.5),
        "hy_f_b3": nrm((N_HYENA, HY_FILTER_WIDTH), 0.02),
        "hy_f_wout": nrm((N_HYENA, HY_FILTER_WIDTH, 2 * D), HY_FILTER_WIDTH ** -0.5),
        "hy_freq": 1.0 + nrm((N_HYENA, HY_FILTER_WIDTH), 0.02),
        "hy_skip": nrm((N_HYENA, D), 1.0),
        "hy_w_out": nrm((N_HYENA, D, D), D ** -0.5),
        "hy_b_out": nrm((N_HYENA, D), 0.02),
        "ret_w_in": nrm((N_RET, D, 4 * D), D ** -0.5),
        "ret_w_out": nrm((N_RET, D, D), D ** -0.5),
        "gla_w_in": nrm((N_GLA, D, 3 * D), D ** -0.5),
        "gla_gate_w1": nrm((N_GLA, 2, D, GLA_GATE_RANK), D ** -0.5),
        "gla_gate_w2": nrm((N_GLA, 2, GLA_GATE_RANK, GLA_KEY_DIM), GLA_GATE_RANK ** -0.5),
        "gla_gate_b": nrm((N_GLA, 2, GLA_KEY_DIM), 0.1),
        "gla_w_out": nrm((N_GLA, D, D), D ** -0.5),
        "norm_g": 1.0 + nrm((DEPTH, 2, D), 0.02),
        "ada_w": nrm((DEPTH, D, 6 * D), 0.5 * D ** -0.5),
        "ada_b": nrm((DEPTH, 6 * D), 0.02),
        "mlp_w1": nrm((DEPTH, D, D_FF), D ** -0.5),
        "mlp_w2": nrm((DEPTH, D_FF, D), D_FF ** -0.5),
        "final_g": 1.0 + nrm((D,), 0.02),
    }


def reference(x_prompt, x_sample, c_prompt, c_sample,
              hy_w_in, hy_b_in, hy_conv_w, hy_conv_b, hy_f_w1, hy_f_b1, hy_f_w2, hy_f_b2,
              hy_f_w3, hy_f_b3, hy_f_wout, hy_freq, hy_skip, hy_w_out, hy_b_out,
              ret_w_in, ret_w_out,
              gla_w_in, gla_gate_w1, gla_gate_w2, gla_gate_b, gla_w_out,
              norm_g, ada_w, ada_b, mlp_w1, mlp_w2, final_g):
    hy = (hy_w_in, hy_b_in, hy_conv_w, hy_conv_b, hy_f_w1, hy_f_b1, hy_f_w2, hy_f_b2,
          hy_f_w3, hy_f_b3, hy_f_wout, hy_freq, hy_skip, hy_w_out, hy_b_out)
    ret = (ret_w_in, ret_w_out)
    gla = (gla_w_in, gla_gate_w1, gla_gate_w2, gla_gate_b, gla_w_out)
    y_prompt = _trunk(x_prompt, c_prompt, hy, ret, gla, norm_g, ada_w, ada_b, mlp_w1, mlp_w2, final_g)
    y_sample = _trunk(x_sample, c_sample, hy, ret, gla, norm_g, ada_w, ada_b, mlp_w1, mlp_w2, final_g)
    return (y_prompt, y_sample)
```

```python
import functools
import math

import jax
import jax.numpy as jnp
from jax import lax
from jax.experimental import pallas as pl
from jax.experimental.pallas import tpu as pltpu

D_MODEL = 4096
DEPTH = 4
N_MIXERS = 3
EPS = 1e-6
HY_BANDS = 16
HY_FAST_DECAY = 0.3
HY_SLOW_DECAY = 1.5
HY_TARGET = 1e-2
HY_SHIFT = 0.05
RET_HEADS = 16
RET_HEAD_DIM = D_MODEL // RET_HEADS
RET_CHUNK = 128
ROPE_BASE = 10000.0
GLA_HEADS = 4
GLA_KEY_DIM = D_MODEL // 2
GLA_DK = GLA_KEY_DIM // GLA_HEADS
GLA_DV = D_MODEL // GLA_HEADS
GLA_TAU = 16.0
GLA_CHUNK = 64

F32 = jnp.float32
BF16 = jnp.bfloat16

V7X_VMEM_LIMIT_BYTES = 56 * 1024 * 1024
MM_BLOCK_M = 1024
MM_BLOCK_N = 1024
MM_BLOCK_K = 4096
MM_OUT = dict(bn=512)
MM_DOWN = dict(bk=2048)


def _mm_kernel(*refs, nk, has_bias, has_resid, act):
    it = iter(refs)
    x_ref, w_ref = next(it), next(it)
    b_ref = next(it) if has_bias else None
    r_ref = next(it) if has_resid else None
    g_ref = next(it) if has_resid else None
    o_ref = next(it)
    acc_ref = next(it) if nk > 1 else None

    part = jnp.dot(x_ref[...].astype(BF16), w_ref[...].astype(BF16), preferred_element_type=F32)

    def finish(acc):
        if has_bias:
            acc = acc + b_ref[...]
        if act == "relu2":
            acc = jnp.maximum(acc, 0.0)
            acc = acc * acc
        if has_resid:
            acc = r_ref[...] + g_ref[...] * acc
        o_ref[...] = acc.astype(o_ref.dtype)

    if nk == 1:
        finish(part)
    else:
        k = pl.program_id(2)

        @pl.when(k == 0)
        def _():
            acc_ref[...] = part

        @pl.when(jnp.logical_and(k > 0, k < nk - 1))
        def _():
            acc_ref[...] += part

        @pl.when(k == nk - 1)
        def _():
            finish(acc_ref[...] + part)


def _matmul(x, w, *, bias=None, act=None, resid=None, gate=None, rows_per_gate=None,
            out_dtype=F32, bm=MM_BLOCK_M, bn=MM_BLOCK_N, bk=MM_BLOCK_K, name="matmul"):
    M, K = x.shape
    K2, N = w.shape
    assert K == K2
    bm, bn, bk = min(bm, M), min(bn, N), min(bk, K)
    assert M % bm == 0 and N % bn == 0 and K % bk == 0
    nk = K // bk
    has_bias = bias is not None
    has_resid = resid is not None

    operands = [x, w]
    in_specs = [pl.BlockSpec((bm, bk), lambda i, j, k: (i, k)),
                pl.BlockSpec((bk, bn), lambda i, j, k: (k, j))]
    if has_bias:
        operands.append(bias.reshape(1, N).astype(F32))
        in_specs.append(pl.BlockSpec((1, bn), lambda i, j, k: (0, j)))
    if has_resid:
        assert rows_per_gate % bm == 0
        tiles_per_gate = rows_per_gate // bm
        operands.append(resid)
        in_specs.append(pl.BlockSpec((bm, bn), lambda i, j, k: (i, j)))
        operands.append(gate.reshape(gate.shape[0], 1, N).astype(F32))
        in_specs.append(pl.BlockSpec((None, 1, bn), lambda i, j, k: (i // tiles_per_gate, 0, j)))

    scratch = [pltpu.VMEM((bm, bn), F32)] if nk > 1 else []
    kern = functools.partial(_mm_kernel, nk=nk, has_bias=has_bias, has_resid=has_resid, act=act)
    return pl.pallas_call(
        kern,
        out_shape=jax.ShapeDtypeStruct((M, N), out_dtype),
        grid=(M // bm, N // bn, nk),
        in_specs=in_specs,
        out_specs=pl.BlockSpec((bm, bn), lambda i, j, k: (i, j)),
        scratch_shapes=scratch,
        compiler_params=pltpu.CompilerParams(
            dimension_semantics=("parallel", "parallel", "arbitrary"),
            vmem_limit_bytes=V7X_VMEM_LIMIT_BYTES),
        name=name,
    )(*operands)


def _rmsnorm(x, g):
    y = x * lax.rsqrt(jnp.mean(x * x, axis=-1, keepdims=True) + EPS)
    return y * g


def _head_rmsnorm(o):
    return o * lax.rsqrt(jnp.mean(o * o, axis=-1, keepdims=True) + EPS)


def _flip(t):
    return t[:, :, ::-1]


def _to_chunks(t, c):
    b, h, l, d = t.shape
    return jnp.moveaxis(t.reshape(b, h, l // c, c, d), 2, 0)


def _from_chunks(t):
    n, b, h, c, d = t.shape
    return jnp.moveaxis(t, 0, 2).reshape(b, h, n * c, d)


def _short_conv(z, w, b):
    zp = jnp.pad(z, ((0, 0), (1, 1), (0, 0)))
    return zp[:, :-2] * w[0] + zp[:, 1:-1] * w[1] + zp[:, 2:] * w[2] + b


def _hyena_filter(L, w1, b1, w2, b2, w3, b3, w_out, freq):
    hp = lax.Precision.HIGHEST
    t = jnp.linspace(0.0, 1.0, L, dtype=F32)[:, None]
    w = (2.0 * math.pi / L) * jnp.arange(L, dtype=F32)[:, None]
    bands = jnp.linspace(1e-4, HY_BANDS - 1, HY_BANDS, dtype=F32)[None, :]
    z = jnp.concatenate([t, jnp.cos(bands * w), -jnp.sin(bands * w)], axis=-1)
    h = jnp.sin(freq * (jnp.dot(z, w1, precision=hp) + b1))
    h = jnp.sin(freq * (jnp.dot(h, w2, precision=hp) + b2))
    h = jnp.sin(freq * (jnp.dot(h, w3, precision=hp) + b3))
    h = jnp.dot(h, w_out, precision=hp)
    max_decay = math.log(HY_TARGET) / HY_FAST_DECAY
    min_decay = math.log(HY_TARGET) / HY_SLOW_DECAY
    deltas = jnp.linspace(min_decay, max_decay, D_MODEL, dtype=F32)
    window = jnp.exp(-t * jnp.abs(deltas)[None, :]) + HY_SHIFT
    h_fwd = h[:, :D_MODEL] * window
    h_bwd = h[:, D_MODEL:] * window
    k = jnp.concatenate([h_fwd, jnp.zeros((1, D_MODEL), F32), h_bwd[:L - 1][::-1]], axis=0)
    return k / (jnp.sum(jnp.abs(k), axis=0, keepdims=True) + EPS)


def _hyena_mixer(h, x, g1, p):
    B, L, D = x.shape
    z = _matmul(h.reshape(B * L, D), p["w_in"], bias=p["b_in"], name="hy_in").reshape(B, L, 3 * D)
    z = _short_conv(z, p["conv_w"], p["conv_b"])
    x0, x1, v = jnp.split(z, 3, axis=-1)
    v = v * x1
    k = _hyena_filter(L, p["f_w1"], p["f_b1"], p["f_w2"], p["f_b2"], p["f_w3"], p["f_b3"],
                      p["f_wout"], p["freq"])
    vf = jnp.fft.rfft(v, n=2 * L, axis=1)
    kf = jnp.fft.rfft(k, axis=0)
    y = jnp.fft.irfft(vf * kf[None], n=2 * L, axis=1)[:, :L]
    y = (y + p["skip"] * v) * x0
    out = _matmul(y.astype(BF16).reshape(B * L, D), p["w_out"], bias=p["b_out"],
                  resid=x.reshape(B * L, D), gate=g1, rows_per_gate=L, name="hy_out", **MM_OUT)
    return out.reshape(B, L, D)


def _rotary(x, pos):
    d = x.shape[-1]
    inv = 1.0 / (ROPE_BASE ** jnp.linspace(0.0, 1.0, d // 2, dtype=F32))
    ang = pos[:, None] * inv[None, :]
    cos = jnp.cos(ang)[None, :, None, :]
    sin = jnp.sin(ang)[None, :, None, :]
    x1, x2 = x[..., 0::2], x[..., 1::2]
    return jnp.stack([x1 * cos - x2 * sin, x1 * sin + x2 * cos], axis=-1).reshape(x.shape)


def _retention_chunked(q, k, v, log_gamma, strict):
    B, H, L, dk = q.shape
    dv = v.shape[-1]
    C = RET_CHUNK
    idx = jnp.arange(C, dtype=F32)
    diff = idx[:, None] - idx[None, :]
    mask = (diff > 0) if strict else (diff >= 0)
    lg = log_gamma[:, None, None]
    intra = jnp.where(mask[None], jnp.exp(lg * jnp.where(mask, diff, 0.0)[None]), 0.0)
    q_decay = jnp.exp(log_gamma[:, None] * (idx + 1.0))[None, :, :, None]
    k_decay = jnp.exp(log_gamma[:, None] * (C - 1.0 - idx))[None, :, :, None]
    chunk_decay = jnp.exp(log_gamma * C)[None, :, None, None]

    def step(state, inp):
        qc, kc, vc = inp
        s = jnp.einsum('bhid,bhjd->bhij', qc, kc) * intra[None]
        o = jnp.einsum('bhij,bhjv->bhiv', s, vc) + jnp.einsum('bhid,bhdv->bhiv', qc * q_decay, state)
        state = state * chunk_decay + jnp.einsum('bhjd,bhjv->bhdv', kc * k_decay, vc)
        return state, o

    state0 = jnp.zeros((B, H, dk, dv), F32)
    _, o = lax.scan(step, state0, (_to_chunks(q, C), _to_chunks(k, C), _to_chunks(v, C)))
    return _from_chunks(o)


def _retention_mixer(h, x, g1, p):
    B, L, D = x.shape
    z = _matmul(h.reshape(B * L, D), p["w_in"], name="ret_in").reshape(B, L, 4 * D)
    q, k, v, g = jnp.split(z, 4, axis=-1)
    pos = jnp.arange(L, dtype=F32)
    heads = lambda t: t.reshape(B, L, RET_HEADS, RET_HEAD_DIM)
    q = _rotary(heads(q), pos)
    k = _rotary(heads(k), pos) * (RET_HEAD_DIM ** -0.5)
    v = heads(v)
    q, k, v = (jnp.transpose(t, (0, 2, 1, 3)) for t in (q, k, v))
    hidx = jnp.arange(RET_HEADS, dtype=F32)
    lg_fwd = jnp.log1p(-jnp.exp2(-5.0 - hidx))
    lg_bwd = jnp.log1p(-jnp.exp2(-5.0 - hidx[::-1]))
    o = (_retention_chunked(q, k, v, lg_fwd, False)
         + _flip(_retention_chunked(_flip(q), _flip(k), _flip(v), lg_bwd, True)))
    o = _head_rmsnorm(o)
    o = jnp.transpose(o, (0, 2, 1, 3)).reshape(B, L, D)
    y = (jax.nn.silu(g) * o).astype(BF16)
    out = _matmul(y.reshape(B * L, D), p["w_out"], resid=x.reshape(B * L, D), gate=g1,
                  rows_per_gate=L, name="ret_out", **MM_OUT)
    return out.reshape(B, L, D)


def _gla_chunked(q, k, v, log_a, strict):
    B, H, L, dk = q.shape
    dv = v.shape[-1]
    C = GLA_CHUNK
    idx = jnp.arange(C)
    mask = (idx[:, None] > idx[None, :]) if strict else (idx[:, None] >= idx[None, :])
    m5 = mask[None, None, :, :, None]

    def step(state, inp):
        qc, kc, vc, gc = inp
        b = jnp.cumsum(gc, axis=2)
        rel = b[:, :, :, None, :] - b[:, :, None, :, :]
        decay = jnp.where(m5, jnp.exp(jnp.where(m5, rel, 0.0)), 0.0)
        s = jnp.einsum('bhid,bhjd,bhijd->bhij', qc, kc, decay)
        o = jnp.einsum('bhij,bhjv->bhiv', s, vc) + jnp.einsum('bhid,bhdv->bhiv', qc * jnp.exp(b), state)
        b_last = b[:, :, -1:, :]
        state = (state * jnp.exp(b_last)[:, :, 0, :, None]
                 + jnp.einsum('bhjd,bhjv->bhdv', kc * jnp.exp(b_last - b), vc))
        return state, o

    state0 = jnp.zeros((B, H, dk, dv), F32)
    _, o = lax.scan(step, state0, (_to_chunks(q, C), _to_chunks(k, C), _to_chunks(v, C),
                                   _to_chunks(log_a, C)))
    return _from_chunks(o)


def _gla_mixer(h, x, g1, p):
    B, L, D = x.shape
    h2 = h.reshape(B * L, D)
    z = _matmul(h2, p["w_in"], name="gla_in").reshape(B, L, 3 * D)
    q, k, v, r = jnp.split(z, [GLA_KEY_DIM, 2 * GLA_KEY_DIM, 2 * GLA_KEY_DIM + D], axis=-1)

    def heads(t, d):
        return jnp.transpose(t.reshape(B, L, GLA_HEADS, d), (0, 2, 1, 3))

    def log_gate(direction):
        zg = (h @ p["gate_w1"][direction]) @ p["gate_w2"][direction] + p["gate_b"][direction]
        return heads(jax.nn.log_sigmoid(zg) / GLA_TAU, GLA_DK)

    q = heads(q, GLA_DK) * (GLA_DK ** -0.5)
    k = heads(k, GLA_DK)
    v = heads(v, GLA_DV)
    o = (_gla_chunked(q, k, v, log_gate(0), False)
         + _flip(_gla_chunked(_flip(q), _flip(k), _flip(v), _flip(log_gate(1)), True)))
    o = _head_rmsnorm(o)
    o = jnp.transpose(o, (0, 2, 1, 3)).reshape(B, L, D)
    y = (jax.nn.silu(r) * o).astype(BF16)
    out = _matmul(y.reshape(B * L, D), p["w_out"], resid=x.reshape(B * L, D), gate=g1,
                  rows_per_gate=L, name="gla_out", **MM_OUT)
    return out.reshape(B, L, D)


def _trunk(x, mod, hy, ret, gla, norm_g, mlp_w1, mlp_w2, final_g):
    B, L, D = x.shape
    for i in range(DEPTH):
        sh1, sc1, g1, sh2, sc2, g2 = jnp.split(mod[i], 6, axis=-1)
        h = (_rmsnorm(x, norm_g[i, 0]) * (1.0 + sc1[:, None]) + sh1[:, None]).astype(BF16)
        kind, slot = i % N_MIXERS, i // N_MIXERS
        if kind == 0:
            x = _hyena_mixer(h, x, g1, {n: v[slot] for n, v in hy.items()})
        elif kind == 1:
            x = _retention_mixer(h, x, g1, {n: v[slot] for n, v in ret.items()})
        else:
            x = _gla_mixer(h, x, g1, {n: v[slot] for n, v in gla.items()})
        h = (_rmsnorm(x, norm_g[i, 1]) * (1.0 + sc2[:, None]) + sh2[:, None]).astype(BF16)
        a = _matmul(h.reshape(B * L, D), mlp_w1[i], act="relu2", out_dtype=BF16, name="mlp_up")
        x = _matmul(a, mlp_w2[i], resid=x.reshape(B * L, D), gate=g2, rows_per_gate=L,
                    name="mlp_down", **MM_DOWN).reshape(B, L, D)
    return _rmsnorm(x, final_g)


def kernel(x_prompt, x_sample, c_prompt, c_sample, hy_w_in, hy_b_in, hy_conv_w, hy_conv_b, hy_f_w1, hy_f_b1, hy_f_w2, hy_f_b2, hy_f_w3, hy_f_b3, hy_f_wout, hy_freq, hy_skip, hy_w_out, hy_b_out, ret_w_in, ret_w_out, gla_w_in, gla_gate_w1, gla_gate_w2, gla_gate_b, gla_w_out, norm_g, ada_w, ada_b, mlp_w1, mlp_w2, final_g):
    bf = lambda w: w.astype(BF16)
    hy = dict(w_in=bf(hy_w_in), b_in=hy_b_in, conv_w=hy_conv_w, conv_b=hy_conv_b,
              f_w1=hy_f_w1, f_b1=hy_f_b1, f_w2=hy_f_w2, f_b2=hy_f_b2, f_w3=hy_f_w3, f_b3=hy_f_b3,
              f_wout=hy_f_wout, freq=hy_freq, skip=hy_skip, w_out=bf(hy_w_out), b_out=hy_b_out)
    ret = dict(w_in=bf(ret_w_in), w_out=bf(ret_w_out))
    gla = dict(w_in=bf(gla_w_in), gate_w1=gla_gate_w1, gate_w2=gla_gate_w2, gate_b=gla_gate_b,
               w_out=bf(gla_w_out))
    w1, w2 = bf(mlp_w1), bf(mlp_w2)

    nb_p, nb_s = c_prompt.shape[0], c_sample.shape[0]
    cs = jax.nn.silu(jnp.concatenate([c_prompt, c_sample], axis=0))
    rows = -(-cs.shape[0] // 16) * 16
    cs = jnp.pad(cs, ((0, rows - cs.shape[0]), (0, 0))).astype(BF16)
    mod = jnp.stack([_matmul(cs, ada_w[i], bias=ada_b[i], bn=512, name="ada") for i in range(DEPTH)])
    mod_p, mod_s = mod[:, :nb_p], mod[:, nb_p:nb_p + nb_s]

    y_prompt = _trunk(x_prompt, mod_p, hy, ret, gla, norm_g, w1, w2, final_g)
    y_sample = _trunk(x_sample, mod_s, hy, ret, gla, norm_g, w1, w2, final_g)
    return (y_prompt, y_sample)
```

```python
import functools
import math

import jax
import jax.numpy as jnp
import numpy as np
from jax import lax
from jax.experimental import pallas as pl
from jax.experimental.pallas import tpu as pltpu

D_MODEL = 4096
DEPTH = 4
N_MIXERS = 3
EPS = 1e-6
HY_BANDS = 16
HY_FAST_DECAY = 0.3
HY_SLOW_DECAY = 1.5
HY_TARGET = 1e-2
HY_SHIFT = 0.05
RET_HEADS = 16
ROPE_BASE = 10000.0
GLA_HEADS = 4
GLA_GATE_RANK = 16
GLA_TAU = 16.0

F32 = jnp.float32
BF16 = jnp.bfloat16

V7X_VMEM_LIMIT_BYTES = 56 * 1024 * 1024
V7X_LANES = 128
MM_BLOCK_M = 1024
MM_BLOCK_N = 1024
MM_BLOCK_K = 4096
MM_OUT = dict(bn=512)
MM_DOWN = dict(bk=2048)
NORM_BLOCK_ROWS = 512
RET_CHUNK = 256
RET_HEADS_PER_STEP = 4
GLA_CHUNK = 128
GLA_SUBBLOCK = 32
GLA_EXP_CLAMP = 80.0


def _cparams(*sem):
    return pltpu.CompilerParams(dimension_semantics=sem, vmem_limit_bytes=V7X_VMEM_LIMIT_BYTES)


def _mm_kernel(*refs, nk, has_bias, has_resid, act):
    it = iter(refs)
    x_ref, w_ref = next(it), next(it)
    b_ref = next(it) if has_bias else None
    r_ref = next(it) if has_resid else None
    g_ref = next(it) if has_resid else None
    o_ref = next(it)
    acc_ref = next(it) if nk > 1 else None

    part = jnp.dot(x_ref[...].astype(BF16), w_ref[...].astype(BF16), preferred_element_type=F32)

    def finish(acc):
        if has_bias:
            acc = acc + b_ref[...]
        if act == "relu2":
            acc = jnp.maximum(acc, 0.0)
            acc = acc * acc
        if has_resid:
            acc = r_ref[...] + g_ref[...] * acc
        o_ref[...] = acc.astype(o_ref.dtype)

    if nk == 1:
        finish(part)
    else:
        k = pl.program_id(2)

        @pl.when(k == 0)
        def _():
            acc_ref[...] = part

        @pl.when(jnp.logical_and(k > 0, k < nk - 1))
        def _():
            acc_ref[...] += part

        @pl.when(k == nk - 1)
        def _():
            finish(acc_ref[...] + part)


def _matmul(x, w, *, bias=None, act=None, resid=None, gate=None, rows_per_gate=None,
            out_dtype=F32, bm=MM_BLOCK_M, bn=MM_BLOCK_N, bk=MM_BLOCK_K, name="matmul"):
    M, K = x.shape
    K2, N = w.shape
    assert K == K2
    bm, bn, bk = min(bm, M, rows_per_gate or M), min(bn, N), min(bk, K)
    assert M % bm == 0 and N % bn == 0 and K % bk == 0
    nk = K // bk
    has_bias = bias is not None
    has_resid = resid is not None

    operands = [x, w]
    in_specs = [pl.BlockSpec((bm, bk), lambda i, j, k: (i, k)),
                pl.BlockSpec((bk, bn), lambda i, j, k: (k, j))]
    if has_bias:
        operands.append(bias.reshape(1, N).astype(F32))
        in_specs.append(pl.BlockSpec((1, bn), lambda i, j, k: (0, j)))
    if has_resid:
        assert rows_per_gate % bm == 0
        tiles_per_gate = rows_per_gate // bm
        operands.append(resid)
        in_specs.append(pl.BlockSpec((bm, bn), lambda i, j, k: (i, j)))
        operands.append(gate.reshape(gate.shape[0], 1, N).astype(F32))
        in_specs.append(pl.BlockSpec((None, 1, bn), lambda i, j, k: (i // tiles_per_gate, 0, j)))

    scratch = [pltpu.VMEM((bm, bn), F32)] if nk > 1 else []
    kern = functools.partial(_mm_kernel, nk=nk, has_bias=has_bias, has_resid=has_resid, act=act)
    return pl.pallas_call(
        kern,
        out_shape=jax.ShapeDtypeStruct((M, N), out_dtype),
        grid=(M // bm, N // bn, nk),
        in_specs=in_specs,
        out_specs=pl.BlockSpec((bm, bn), lambda i, j, k: (i, j)),
        scratch_shapes=scratch,
        compiler_params=_cparams("parallel", "parallel", "arbitrary"),
        name=name,
    )(*operands)


def _norm_kernel(x_ref, g_ref, *rest, modulate):
    if modulate:
        sc_ref, sh_ref, o_ref = rest
    else:
        (o_ref,) = rest
    x = x_ref[...]
    y = x * lax.rsqrt(jnp.mean(x * x, axis=-1, keepdims=True) + EPS) * g_ref[...]
    if modulate:
        y = y * (1.0 + sc_ref[...]) + sh_ref[...]
    o_ref[...] = y.astype(o_ref.dtype)


def _norm(x, g, scale=None, shift=None, *, out_dtype, name):
    B, L, D = x.shape
    rows = min(NORM_BLOCK_ROWS, L)
    assert L % rows == 0
    modulate = scale is not None
    operands = [x, g.reshape(1, D)]
    in_specs = [pl.BlockSpec((None, rows, D), lambda b, i: (b, i, 0)),
                pl.BlockSpec((1, D), lambda b, i: (0, 0))]
    if modulate:
        operands += [scale.reshape(B, 1, D), shift.reshape(B, 1, D)]
        in_specs += [pl.BlockSpec((None, 1, D), lambda b, i: (b, 0, 0))] * 2
    return pl.pallas_call(
        functools.partial(_norm_kernel, modulate=modulate),
        out_shape=jax.ShapeDtypeStruct((B, L, D), out_dtype),
        grid=(B, L // rows),
        in_specs=in_specs,
        out_specs=pl.BlockSpec((None, rows, D), lambda b, i: (b, i, 0)),
        compiler_params=_cparams("parallel", "parallel"),
        name=name,
    )(*operands)


def _silu(x):
    return x / (1.0 + jnp.exp(-x))


def _gated_head_norm(o, gate):
    o = o * lax.rsqrt(jnp.mean(o * o, axis=-1, keepdims=True) + EPS)
    return _silu(gate) * o


def _ret_kernel(lg_ref, q_ref, k_ref, v_ref, cos_ref, sin_ref, *rest, reverse, final, heads_per_step, hd):
    if final:
        op_ref, g_ref, o_ref, state_ref = rest
    else:
        o_ref, state_ref = rest
    C = q_ref.shape[0]
    half = hd // 2

    @pl.when(pl.program_id(2) == 0)
    def _():
        state_ref[...] = jnp.zeros_like(state_ref)

    cos, sin = cos_ref[...], sin_ref[...]
    row = lax.broadcasted_iota(jnp.int32, (C, C), 0)
    col = lax.broadcasted_iota(jnp.int32, (C, C), 1)
    ridx = lax.broadcasted_iota(jnp.int32, (C, 1), 0).astype(F32)
    if reverse:
        mask = col > row
        dist = (col - row).astype(F32)
        q_pow, k_pow = C - ridx, ridx
    else:
        mask = col <= row
        dist = (row - col).astype(F32)
        q_pow, k_pow = ridx + 1.0, C - 1.0 - ridx
    dist = jnp.where(mask, dist, 0.0)

    def rotate(x):
        x1, x2 = x[:, :half], x[:, half:]
        return jnp.concatenate([x1 * cos - x2 * sin, x1 * sin + x2 * cos], axis=1)

    for hh in range(heads_per_step):
        lg = lg_ref[pl.program_id(1) * heads_per_step + hh]
        sl = slice(hh * hd, (hh + 1) * hd)
        q = rotate(q_ref[:, sl])
        k = rotate(k_ref[:, sl]) * (hd ** -0.5)
        v = v_ref[:, sl].astype(BF16)
        intra = jnp.where(mask, jnp.exp(lg * dist), 0.0)
        s = lax.dot_general(q.astype(BF16), k.astype(BF16), (((1,), (1,)), ((), ())),
                            preferred_element_type=F32) * intra
        state = state_ref[hh]
        o = (jnp.dot(s.astype(BF16), v, preferred_element_type=F32)
             + jnp.dot((q * jnp.exp(lg * q_pow)).astype(BF16), state.astype(BF16),
                       preferred_element_type=F32))
        kd = (k * jnp.exp(lg * k_pow)).astype(BF16)
        state_ref[hh] = state * jnp.exp(lg * jnp.full((1, hd), C, F32)) + lax.dot_general(
            kd, v, (((0,), (0,)), ((), ())), preferred_element_type=F32)
        if final:
            o_ref[:, sl] = _gated_head_norm(op_ref[:, sl] + o, g_ref[:, sl]).astype(o_ref.dtype)
        else:
            o_ref[:, sl] = o


def _ret_pass(z, log_gamma, cos, sin, o_prev, *, reverse):
    B, L, D4 = z.shape
    D = D4 // 4
    hd = D // RET_HEADS
    hps = RET_HEADS_PER_STEP
    W = hps * hd
    C = min(RET_CHUNK, L)
    n = L // C
    nblk = D // W
    final = o_prev is not None
    cidx = (lambda c: n - 1 - c) if reverse else (lambda c: c)

    def zspec(part):
        return pl.BlockSpec((None, C, W), lambda b, h, c, lg: (b, cidx(c), part * nblk + h))

    tspec = pl.BlockSpec((C, hd // 2), lambda b, h, c, lg: (cidx(c), 0))
    ospec = pl.BlockSpec((None, C, W), lambda b, h, c, lg: (b, cidx(c), h))
    operands = [z, z, z, cos, sin]
    in_specs = [zspec(0), zspec(1), zspec(2), tspec, tspec]
    if final:
        operands += [o_prev, z]
        in_specs += [ospec, zspec(3)]
    kern = functools.partial(_ret_kernel, reverse=reverse, final=final, heads_per_step=hps, hd=hd)
    return pl.pallas_call(
        kern,
        out_shape=jax.ShapeDtypeStruct((B, L, D), BF16 if final else F32),
        grid_spec=pltpu.PrefetchScalarGridSpec(
            num_scalar_prefetch=1,
            grid=(B, nblk, n),
            in_specs=in_specs,
            out_specs=ospec,
            scratch_shapes=[pltpu.VMEM((hps, hd, hd), F32)]),
        compiler_params=_cparams("parallel", "parallel", "arbitrary"),
        name="ret_bwd" if reverse else "ret_fwd",
    )(log_gamma, *operands)


def _pair_split_perm(d, heads):
    hd = d // heads
    idx = np.arange(d).reshape(heads, hd)
    return np.concatenate([idx[:, 0::2], idx[:, 1::2]], axis=1).reshape(-1)


def _retention_mixer(h, x, g1, p):
    B, L, D = x.shape
    hd = D // RET_HEADS
    z = _matmul(h.reshape(B * L, D), p["w_in"], name="ret_in").reshape(B, L, 4 * D)
    pos = jnp.arange(L, dtype=F32)
    inv = 1.0 / (ROPE_BASE ** jnp.linspace(0.0, 1.0, hd // 2, dtype=F32))
    ang = pos[:, None] * inv[None, :]
    cos, sin = jnp.cos(ang), jnp.sin(ang)
    hidx = jnp.arange(RET_HEADS, dtype=F32)
    lg_fwd = jnp.log1p(-jnp.exp2(-5.0 - hidx))
    lg_bwd = jnp.log1p(-jnp.exp2(-5.0 - hidx[::-1]))
    o_fwd = _ret_pass(z, lg_fwd, cos, sin, None, reverse=False)
    y = _ret_pass(z, lg_bwd, cos, sin, o_fwd, reverse=True)
    out = _matmul(y.reshape(B * L, D), p["w_out"], resid=x.reshape(B * L, D), gate=g1,
                  rows_per_gate=L, name="ret_out", **MM_OUT)
    return out.reshape(B, L, D)


def _gla_kernel(q_ref, k_ref, v_ref, t_ref, w2_ref, gb_ref, *rest, reverse, final, sub):
    if final:
        op_ref, r_ref, o_ref, state_ref, a_ref = rest
    else:
        o_ref, state_ref, a_ref = rest
    C, dk = q_ref.shape

    @pl.when(pl.program_id(2) == 0)
    def _():
        state_ref[...] = jnp.zeros_like(state_ref)

    zg = jnp.dot(t_ref[...].astype(BF16), w2_ref[...].astype(BF16), preferred_element_type=F32) + gb_ref[...]
    log_a = (jnp.minimum(zg, 0.0) - jnp.log1p(jnp.exp(-jnp.abs(zg)))) * (1.0 / GLA_TAU)

    row = lax.broadcasted_iota(jnp.int32, (C, C), 0)
    col = lax.broadcasted_iota(jnp.int32, (C, C), 1)
    tri = ((col >= row) if reverse else (col <= row)).astype(BF16)
    hi = log_a.astype(BF16)
    lo = (log_a - hi.astype(F32)).astype(BF16)
    b = (jnp.dot(tri, hi, preferred_element_type=F32) + jnp.dot(tri, lo, preferred_element_type=F32))
    b_tot = jnp.sum(log_a, axis=0, keepdims=True)

    q = q_ref[...] * (dk ** -0.5)
    k = k_ref[...]
    v = v_ref[...].astype(BF16)

    for blk in range(C // sub):
        rows = slice(blk * sub, (blk + 1) * sub)
        ref = b[blk * sub + sub // 2:blk * sub + sub // 2 + 1, :]
        qh = q[rows] * jnp.exp(jnp.minimum(b[rows] - ref, GLA_EXP_CLAMP))
        kh = k * jnp.exp(jnp.minimum(ref - b, GLA_EXP_CLAMP))
        s = lax.dot_general(qh.astype(BF16), kh.astype(BF16), (((1,), (1,)), ((), ())),
                            preferred_element_type=F32)
        r_i = lax.broadcasted_iota(jnp.int32, (sub, C), 0) + blk * sub
        c_i = lax.broadcasted_iota(jnp.int32, (sub, C), 1)
        keep = (c_i > r_i) if reverse else (c_i <= r_i)
        a_ref[rows, :] = jnp.where(keep, s, 0.0).astype(BF16)

    state = state_ref[...]
    o = (jnp.dot(a_ref[...], v, preferred_element_type=F32)
         + lax.dot_general((q * jnp.exp(b)).astype(BF16), state.astype(BF16), (((1,), (1,)), ((), ())),
                           preferred_element_type=F32))
    kd = (k * jnp.exp(b_tot - b)).astype(BF16)
    state_ref[...] = state * jnp.exp(b_tot) + lax.dot_general(
        v, kd, (((0,), (0,)), ((), ())), preferred_element_type=F32)
    if final:
        o_ref[...] = _gated_head_norm(op_ref[...] + o, r_ref[...]).astype(o_ref.dtype)
    else:
        o_ref[...] = o


def _gla_pass(z, t, w2, gb, o_prev, *, direction):
    B, L, D3 = z.shape
    D = D3 // 3
    dk = D // 2 // GLA_HEADS
    dv = D // GLA_HEADS
    C = min(GLA_CHUNK, L)
    n = L // C
    reverse = direction == 1
    final = o_prev is not None
    cidx = (lambda c: n - 1 - c) if reverse else (lambda c: c)
    H = GLA_HEADS

    qspec = pl.BlockSpec((None, C, dk), lambda b, h, c: (b, cidx(c), h))
    kspec = pl.BlockSpec((None, C, dk), lambda b, h, c: (b, cidx(c), H + h))
    vspec = pl.BlockSpec((None, C, dv), lambda b, h, c: (b, cidx(c), H + h))
    rspec = pl.BlockSpec((None, C, dv), lambda b, h, c: (b, cidx(c), 2 * H + h))
    ospec = pl.BlockSpec((None, C, dv), lambda b, h, c: (b, cidx(c), h))
    tspec = pl.BlockSpec((None, C, t.shape[-1]), lambda b, h, c: (b, cidx(c), 0))
    w2spec = pl.BlockSpec((None, t.shape[-1], dk), lambda b, h, c: (direction, 0, h))
    gbspec = pl.BlockSpec((None, 1, dk), lambda b, h, c: (direction, 0, h))
    operands = [z, z, z, t, w2, gb]
    in_specs = [qspec, kspec, vspec, tspec, w2spec, gbspec]
    if final:
        operands += [o_prev, z]
        in_specs += [ospec, rspec]
    kern = functools.partial(_gla_kernel, reverse=reverse, final=final, sub=min(GLA_SUBBLOCK, C))
    return pl.pallas_call(
        kern,
        out_shape=jax.ShapeDtypeStruct((B, L, D), BF16 if final else F32),
        grid=(B, H, n),
        in_specs=in_specs,
        out_specs=ospec,
        scratch_shapes=[pltpu.VMEM((dv, dk), F32), pltpu.VMEM((C, C), BF16)],
        compiler_params=_cparams("parallel", "parallel", "arbitrary"),
        name="gla_bwd" if reverse else "gla_fwd",
    )(*operands)


def _gla_mixer(h, x, g1, p):
    B, L, D = x.shape
    h2 = h.reshape(B * L, D)
    z = _matmul(h2, p["w_in"], name="gla_in").reshape(B, L, 3 * D)
    t = _matmul(h2, p["gate_w1"], bn=V7X_LANES, name="gla_gate").reshape(B, L, V7X_LANES)
    o_fwd = _gla_pass(z, t, p["gate_w2"], p["gate_b"], None, direction=0)
    y = _gla_pass(z, t, p["gate_w2"], p["gate_b"], o_fwd, direction=1)
    out = _matmul(y.reshape(B * L, D), p["w_out"], resid=x.reshape(B * L, D), gate=g1,
                  rows_per_gate=L, name="gla_out", **MM_OUT)
    return out.reshape(B, L, D)


def _conv_gate_kernel(*refs, rows):
    groups = [refs[5 * g:5 * g + 5] for g in range(3)]
    x0_out, vv_out, pad_ref = refs[15:]
    i = pl.program_id(1)
    first = i == 0
    last = i == pl.num_programs(1) - 1

    def conv(cur_ref, prev_ref, next_ref, w_ref, b_ref):
        pad_ref[pl.ds(8, rows), :] = cur_ref[...]
        pad_ref[pl.ds(0, 8), :] = jnp.where(first, 0.0, prev_ref[...])
        pad_ref[pl.ds(rows + 8, 8), :] = jnp.where(last, 0.0, next_ref[...])
        w = w_ref[...]
        return (pad_ref[pl.ds(7, rows), :] * w[0:1] + cur_ref[...] * w[1:2]
                + pad_ref[pl.ds(9, rows), :] * w[2:3] + b_ref[...])

    x0_out[...] = conv(*groups[0])
    x1 = conv(*groups[1])
    vv_out[...] = conv(*groups[2]) * x1


def _conv_gate(z, conv_w, conv_b):
    B, L, D3 = z.shape
    D = D3 // 3
    rows, ct = min(512, L), 512
    nct = D // ct
    nhalo = L // 8
    operands, in_specs = [], []
    for g in range(3):
        operands += [z, z, z, conv_w, conv_b.reshape(1, D3)]
        in_specs += [
            pl.BlockSpec((None, rows, ct), lambda b, i, j, g=g: (b, i, g * nct + j)),
            pl.BlockSpec((None, 8, ct), lambda b, i, j, g=g: (b, jnp.maximum(i * (rows // 8) - 1, 0), g * nct + j)),
            pl.BlockSpec((None, 8, ct), lambda b, i, j, g=g: (b, jnp.minimum((i + 1) * (rows // 8), nhalo - 1), g * nct + j)),
            pl.BlockSpec((3, ct), lambda b, i, j, g=g: (0, g * nct + j)),
            pl.BlockSpec((1, ct), lambda b, i, j, g=g: (0, g * nct + j)),
        ]
    ospec = pl.BlockSpec((None, rows, ct), lambda b, i, j: (b, i, j))
    return pl.pallas_call(
        functools.partial(_conv_gate_kernel, rows=rows),
        out_shape=(jax.ShapeDtypeStruct((B, L, D), F32), jax.ShapeDtypeStruct((B, L, D), F32)),
        grid=(B, L // rows, nct),
        in_specs=in_specs,
        out_specs=(ospec, ospec),
        scratch_shapes=[pltpu.VMEM((rows + 16, ct), F32)],
        compiler_params=_cparams("parallel", "parallel", "parallel"),
        name="hy_conv_gate",
    )(*operands)


FFT_GROUP = 16
FFT_K1_GROUP = 8
FFT_LANES = 128


def _fft_conv_kernel(x_ref, x0_ref, g_ref, f_ref, fi_ref, kf_ref, gi_ref, skip_ref, o_ref, z_ref, y_ref,
                     *, sa, sb, n1, n2, h1):
    step = pl.program_id(2)
    ng = n2 // FFT_GROUP

    @pl.when(step < sa)
    def _():
        for j in range(FFT_GROUP):
            x = jnp.concatenate([x_ref[0, :, j, :], x_ref[1, :, j, :]], axis=0).astype(BF16)
            a = jnp.dot(g_ref[j], x, preferred_element_type=F32)
            z_ref[:, step, j, :] = a[:n1]
            z_ref[:, ng + step, j, :] = a[n1:]

    @pl.when(jnp.logical_and(step >= sa, step < sa + sb))
    def _():
        for j in range(FFT_K1_GROUP):
            k1 = (step - sa) * FFT_K1_GROUP + j
            zk = z_ref[k1].reshape(2 * n2, z_ref.shape[-1]).astype(BF16)
            y = jnp.dot(f_ref[...], zk, preferred_element_type=F32)
            yr, yi = y[:n2], y[n2:]
            kr, ki = kf_ref[j, :n2, :], kf_ref[j, n2:, :]
            p = jnp.concatenate([yr * kr - yi * ki, yr * ki + yi * kr], axis=0).astype(BF16)
            z_ref[k1] = jnp.dot(fi_ref[...], p, preferred_element_type=F32).reshape(z_ref.shape[1:])

    @pl.when(step >= sa + sb)
    def _():
        g = step - sa - sb
        for j in range(FFT_GROUP):
            bz = jnp.concatenate([z_ref[:, g, j, :], z_ref[:, ng + g, j, :]], axis=0).astype(BF16)
            y = jnp.dot(gi_ref[j], bz, preferred_element_type=F32)
            y_ref[0, :, j, :] = y[:h1]
            y_ref[1, :, j, :] = y[h1:]
        o_ref[...] = ((y_ref[...] + skip_ref[...] * x_ref[...]) * x0_ref[...]).astype(o_ref.dtype)


def _dft_tables(L):
    N = 2 * L
    n1 = n2 = int(round(math.sqrt(N)))
    assert n1 * n2 == N and n2 % FFT_GROUP == 0 and n1 % FFT_K1_GROUP == 0
    h1 = n1 // 2
    k1 = jnp.arange(n1, dtype=jnp.int32)
    m = (jnp.arange(n2, dtype=jnp.int32)[:, None, None] * k1[None, :, None]
         + n2 * k1[None, :, None] * jnp.arange(h1, dtype=jnp.int32)[None, None, :]) % N
    ang = (2.0 * math.pi / N) * m.astype(F32)
    c, s = jnp.cos(ang), jnp.sin(ang)
    block = lambda re, im: jnp.concatenate([jnp.concatenate([re, -im], -1), jnp.concatenate([im, re], -1)], -2)
    g = block(c, -s).astype(BF16)
    ct_, st_ = jnp.swapaxes(c, 1, 2) / N, jnp.swapaxes(s, 1, 2) / N
    gi = block(ct_, st_).astype(BF16)
    idx = jnp.arange(n2, dtype=jnp.int32)
    ang2 = (2.0 * math.pi / n2) * ((idx[:, None] * idx[None, :]) % n2).astype(F32)
    c2, s2 = jnp.cos(ang2), jnp.sin(ang2)
    return dict(n1=n1, n2=n2, h1=h1, g=g, gi=gi, f=block(c2, -s2).astype(BF16), fi=block(c2, s2).astype(BF16))


def _filter_spectrum(k, tab):
    n1, n2 = tab["n1"], tab["n2"]
    kf = jnp.fft.fft(k, axis=0).reshape(n2, n1, -1).transpose(1, 0, 2)
    return jnp.concatenate([jnp.real(kf), jnp.imag(kf)], axis=1).astype(F32)


def _fft_conv(vv, x0, kf, skip, tab):
    B, L, D = vv.shape
    n1, n2, h1 = tab["n1"], tab["n2"], tab["h1"]
    assert B % 2 == 0 and L == h1 * n2
    ct = FFT_LANES
    sa = n2 // FFT_GROUP
    sb = n1 // FFT_K1_GROUP
    sc = sa
    shape5 = (B // 2, 2, h1, n2, D)
    a_idx = lambda s: jnp.where(s < sa, s, jnp.where(s >= sa + sb, s - sa - sb, sa - 1))
    c_idx = lambda s: jnp.clip(s - sa - sb, 0, sc - 1)
    blk = (None, 2, h1, FFT_GROUP, ct)
    in_specs = [
        pl.BlockSpec(blk, lambda p, t, s: (p, 0, 0, a_idx(s), t)),
        pl.BlockSpec(blk, lambda p, t, s: (p, 0, 0, c_idx(s), t)),
        pl.BlockSpec((FFT_GROUP, 2 * n1, 2 * h1), lambda p, t, s: (jnp.minimum(s, sa - 1), 0, 0)),
        pl.BlockSpec((2 * n2, 2 * n2), lambda p, t, s: (0, 0)),
        pl.BlockSpec((2 * n2, 2 * n2), lambda p, t, s: (0, 0)),
        pl.BlockSpec((FFT_K1_GROUP, 2 * n2, ct), lambda p, t, s: (jnp.clip(s - sa, 0, sb - 1), 0, t)),
        pl.BlockSpec((FFT_GROUP, 2 * h1, 2 * n1), lambda p, t, s: (c_idx(s), 0, 0)),
        pl.BlockSpec((1, ct), lambda p, t, s: (0, t)),
    ]
    kern = functools.partial(_fft_conv_kernel, sa=sa, sb=sb, n1=n1, n2=n2, h1=h1)
    y = pl.pallas_call(
        kern,
        out_shape=jax.ShapeDtypeStruct(shape5, BF16),
        grid=(B // 2, D // ct, sa + sb + sc),
        in_specs=in_specs,
        out_specs=pl.BlockSpec(blk, lambda p, t, s: (p, 0, 0, c_idx(s), t)),
        scratch_shapes=[pltpu.VMEM((n1, 2 * n2 // FFT_GROUP, FFT_GROUP, ct), F32),
                        pltpu.VMEM((2, h1, FFT_GROUP, ct), F32)],
        compiler_params=_cparams("parallel", "parallel", "arbitrary"),
        name="hy_fft_conv",
    )(vv.reshape(shape5), x0.reshape(shape5), tab["g"], tab["f"], tab["fi"], kf, tab["gi"],
      skip.reshape(1, D))
    return y.reshape(B, L, D)


def _hyena_filter(L, w1, b1, w2, b2, w3, b3, w_out, freq):
    hp = lax.Precision.HIGHEST
    D = w_out.shape[-1] // 2
    t = jnp.linspace(0.0, 1.0, L, dtype=F32)[:, None]
    w = (2.0 * math.pi / L) * jnp.arange(L, dtype=F32)[:, None]
    bands = jnp.linspace(1e-4, HY_BANDS - 1, HY_BANDS, dtype=F32)[None, :]
    z = jnp.concatenate([t, jnp.cos(bands * w), -jnp.sin(bands * w)], axis=-1)
    h = jnp.sin(freq * (jnp.dot(z, w1, precision=hp) + b1))
    h = jnp.sin(freq * (jnp.dot(h, w2, precision=hp) + b2))
    h = jnp.sin(freq * (jnp.dot(h, w3, precision=hp) + b3))
    h = jnp.dot(h, w_out, precision=hp)
    max_decay = math.log(HY_TARGET) / HY_FAST_DECAY
    min_decay = math.log(HY_TARGET) / HY_SLOW_DECAY
    deltas = jnp.linspace(min_decay, max_decay, D, dtype=F32)
    window = jnp.exp(-t * jnp.abs(deltas)[None, :]) + HY_SHIFT
    h_fwd = h[:, :D] * window
    h_bwd = h[:, D:] * window
    k = jnp.concatenate([h_fwd, jnp.zeros((1, D), F32), h_bwd[:L - 1][::-1]], axis=0)
    return k / (jnp.sum(jnp.abs(k), axis=0, keepdims=True) + EPS)


def _hyena_mixer(h, x, g1, p, tab):
    B, L, D = x.shape
    z = _matmul(h.reshape(B * L, D), p["w_in"], bias=p["b_in"], name="hy_in").reshape(B, L, 3 * D)
    x0, vv = _conv_gate(z, p["conv_w"], p["conv_b"])
    k = _hyena_filter(L, p["f_w1"], p["f_b1"], p["f_w2"], p["f_b2"], p["f_w3"], p["f_b3"],
                      p["f_wout"], p["freq"])
    y = _fft_conv(vv, x0, _filter_spectrum(k, tab), p["skip"], tab)
    out = _matmul(y.reshape(B * L, D), p["w_out"], bias=p["b_out"],
                  resid=x.reshape(B * L, D), gate=g1, rows_per_gate=L, name="hy_out", **MM_OUT)
    return out.reshape(B, L, D)


def _trunk(x, mod, hy, ret, gla, norm_g, mlp_w1, mlp_w2, final_g):
    B, L, D = x.shape
    tab = _dft_tables(L)
    for i in range(DEPTH):
        sh1, sc1, g1, sh2, sc2, g2 = jnp.split(mod[i], 6, axis=-1)
        h = _norm(x, norm_g[i, 0], sc1, sh1, out_dtype=BF16, name="norm_mix")
        kind, slot = i % N_MIXERS, i // N_MIXERS
        if kind == 0:
            x = _hyena_mixer(h, x, g1, {n: v[slot] for n, v in hy.items()}, tab)
        elif kind == 1:
            x = _retention_mixer(h, x, g1, {n: v[slot] for n, v in ret.items()})
        else:
            x = _gla_mixer(h, x, g1, {n: v[slot] for n, v in gla.items()})
        h = _norm(x, norm_g[i, 1], sc2, sh2, out_dtype=BF16, name="norm_mlp")
        a = _matmul(h.reshape(B * L, D), mlp_w1[i], act="relu2", out_dtype=BF16, name="mlp_up")
        x = _matmul(a, mlp_w2[i], resid=x.reshape(B * L, D), gate=g2, rows_per_gate=L,
                    name="mlp_down", **MM_DOWN).reshape(B, L, D)
    return _norm(x, final_g, out_dtype=F32, name="norm_final")


def kernel(x_prompt, x_sample, c_prompt, c_sample, hy_w_in, hy_b_in, hy_conv_w, hy_conv_b, hy_f_w1, hy_f_b1, hy_f_w2, hy_f_b2, hy_f_w3, hy_f_b3, hy_f_wout, hy_freq, hy_skip, hy_w_out, hy_b_out, ret_w_in, ret_w_out, gla_w_in, gla_gate_w1, gla_gate_w2, gla_gate_b, gla_w_out, norm_g, ada_w, ada_b, mlp_w1, mlp_w2, final_g):
    D = x_prompt.shape[-1]
    bf = lambda w: w.astype(BF16)
    hy = dict(w_in=bf(hy_w_in), b_in=hy_b_in, conv_w=hy_conv_w, conv_b=hy_conv_b,
              f_w1=hy_f_w1, f_b1=hy_f_b1, f_w2=hy_f_w2, f_b2=hy_f_b2, f_w3=hy_f_w3, f_b3=hy_f_b3,
              f_wout=hy_f_wout, freq=hy_freq, skip=hy_skip, w_out=bf(hy_w_out), b_out=hy_b_out)

    perm = _pair_split_perm(D, RET_HEADS)
    cols = np.concatenate([perm, D + perm, np.arange(2 * D, 4 * D)])
    ret = dict(w_in=bf(ret_w_in[:, :, cols]), w_out=bf(ret_w_out))

    n_gla, _, _, rank = gla_gate_w1.shape
    w1 = jnp.concatenate([gla_gate_w1[:, 0], gla_gate_w1[:, 1],
                          jnp.zeros((n_gla, D, V7X_LANES - 2 * rank), F32)], axis=-1)
    w2 = jnp.zeros((n_gla, 2, V7X_LANES, gla_gate_w2.shape[-1]), F32)
    w2 = w2.at[:, 0, :rank].set(gla_gate_w2[:, 0]).at[:, 1, rank:2 * rank].set(gla_gate_w2[:, 1])
    gla = dict(w_in=bf(gla_w_in), gate_w1=bf(w1), gate_w2=w2, gate_b=gla_gate_b[:, :, None, :],
               w_out=bf(gla_w_out))
    mw1, mw2 = bf(mlp_w1), bf(mlp_w2)

    nb_p, nb_s = c_prompt.shape[0], c_sample.shape[0]
    cs = jax.nn.silu(jnp.concatenate([c_prompt, c_sample], axis=0))
    rows = -(-cs.shape[0] // 16) * 16
    cs = jnp.pad(cs, ((0, rows - cs.shape[0]), (0, 0))).astype(BF16)
    mod = jnp.stack([_matmul(cs, ada_w[i], bias=ada_b[i], bn=512, name="ada") for i in range(DEPTH)])
    mod_p, mod_s = mod[:, :nb_p], mod[:, nb_p:nb_p + nb_s]

    y_prompt = _trunk(x_prompt, mod_p, hy, ret, gla, norm_g, mw1, mw2, final_g)
    y_sample = _trunk(x_sample, mod_s, hy, ret, gla, norm_g, mw1, mw2, final_g)
    return (y_prompt, y_sample)
```

```python
import functools
import math

import jax
import jax.numpy as jnp
import numpy as np
from jax import lax
from jax.experimental import pallas as pl
from jax.experimental.pallas import tpu as pltpu

D_MODEL = 4096
DEPTH = 4
N_MIXERS = 3
EPS = 1e-6
HY_BANDS = 16
HY_FAST_DECAY = 0.3
HY_SLOW_DECAY = 1.5
HY_TARGET = 1e-2
HY_SHIFT = 0.05
RET_HEADS = 16
ROPE_BASE = 10000.0
GLA_HEADS = 4
GLA_GATE_RANK = 16
GLA_TAU = 16.0

F32 = jnp.float32
BF16 = jnp.bfloat16

V7X_VMEM_LIMIT_BYTES = 56 * 1024 * 1024
V7X_LANES = 128
MM_BLOCK_M = 1024
MM_BLOCK_N = 1024
MM_BLOCK_K = 4096
MM_OUT = dict(bn=512)
NORM_BLOCK_ROWS = 512
RET_CHUNK = 256
RET_HEADS_PER_STEP = 4
GLA_CHUNK = 128
GLA_SUBBLOCK = 32
GLA_EXP_CLAMP = 80.0


def _cparams(*sem):
    return pltpu.CompilerParams(dimension_semantics=sem, vmem_limit_bytes=V7X_VMEM_LIMIT_BYTES)


def _mm_kernel(*refs, nk, has_bias, has_resid, act):
    it = iter(refs)
    x_ref, w_ref = next(it), next(it)
    b_ref = next(it) if has_bias else None
    r_ref = next(it) if has_resid else None
    g_ref = next(it) if has_resid else None
    o_ref = next(it)
    acc_ref = next(it) if nk > 1 else None

    part = jnp.dot(x_ref[...].astype(BF16), w_ref[...].astype(BF16), preferred_element_type=F32)

    def finish(acc):
        if has_bias:
            acc = acc + b_ref[...]
        if act == "relu2":
            acc = jnp.maximum(acc, 0.0)
            acc = acc * acc
        if has_resid:
            acc = r_ref[...] + g_ref[...] * acc
        o_ref[...] = acc.astype(o_ref.dtype)

    if nk == 1:
        finish(part)
    else:
        k = pl.program_id(2)

        @pl.when(k == 0)
        def _():
            acc_ref[...] = part

        @pl.when(jnp.logical_and(k > 0, k < nk - 1))
        def _():
            acc_ref[...] += part

        @pl.when(k == nk - 1)
        def _():
            finish(acc_ref[...] + part)


def _matmul(x, w, *, bias=None, act=None, resid=None, gate=None, rows_per_gate=None,
            out_dtype=F32, bm=MM_BLOCK_M, bn=MM_BLOCK_N, bk=MM_BLOCK_K, name="matmul"):
    M, K = x.shape
    K2, N = w.shape
    assert K == K2
    bm, bn, bk = min(bm, M, rows_per_gate or M), min(bn, N), min(bk, K)
    assert M % bm == 0 and N % bn == 0 and K % bk == 0
    nk = K // bk
    has_bias = bias is not None
    has_resid = resid is not None

    operands = [x, w]
    in_specs = [pl.BlockSpec((bm, bk), lambda i, j, k: (i, k)),
                pl.BlockSpec((bk, bn), lambda i, j, k: (k, j))]
    if has_bias:
        operands.append(bias.reshape(1, N).astype(F32))
        in_specs.append(pl.BlockSpec((1, bn), lambda i, j, k: (0, j)))
    if has_resid:
        assert rows_per_gate % bm == 0
        tiles_per_gate = rows_per_gate // bm
        operands.append(resid)
        in_specs.append(pl.BlockSpec((bm, bn), lambda i, j, k: (i, j)))
        operands.append(gate.reshape(gate.shape[0], 1, N).astype(F32))
        in_specs.append(pl.BlockSpec((None, 1, bn), lambda i, j, k: (i // tiles_per_gate, 0, j)))

    scratch = [pltpu.VMEM((bm, bn), F32)] if nk > 1 else []
    kern = functools.partial(_mm_kernel, nk=nk, has_bias=has_bias, has_resid=has_resid, act=act)
    return pl.pallas_call(
        kern,
        out_shape=jax.ShapeDtypeStruct((M, N), out_dtype),
        grid=(M // bm, N // bn, nk),
        in_specs=in_specs,
        out_specs=pl.BlockSpec((bm, bn), lambda i, j, k: (i, j)),
        scratch_shapes=scratch,
        compiler_params=_cparams("parallel", "parallel", "arbitrary"),
        name=name,
    )(*operands)


def _mm_deep_kernel(x_ref, w_ref, r_ref, g_ref, o_ref, acc_ref, *, nk, bn):
    k = pl.program_id(1)
    cols = pl.ds(pl.multiple_of(pl.program_id(2) * bn, bn), bn)
    part = jnp.dot(x_ref[...], w_ref[...], preferred_element_type=F32)

    @pl.when(k == 0)
    def _():
        acc_ref[:, cols] = part

    @pl.when(jnp.logical_and(k > 0, k < nk - 1))
    def _():
        acc_ref[:, cols] += part

    @pl.when(k == nk - 1)
    def _():
        o_ref[...] = r_ref[...] + g_ref[...] * (acc_ref[:, cols] + part)


def _matmul_deep(x, w, resid, gate, rows_per_gate, *, bm=MM_BLOCK_M, bn=MM_BLOCK_N, bk=2048, name):
    M, K = x.shape
    N = w.shape[1]
    bm = min(bm, M, rows_per_gate)
    assert M % bm == 0 and N % bn == 0 and K % bk == 0 and rows_per_gate % bm == 0
    nk = K // bk
    assert nk >= 2
    tiles_per_gate = rows_per_gate // bm
    jcol = lambda k, j: jnp.where(k == nk - 1, j, 0)
    return pl.pallas_call(
        functools.partial(_mm_deep_kernel, nk=nk, bn=bn),
        out_shape=jax.ShapeDtypeStruct((M, N), F32),
        grid=(M // bm, nk, N // bn),
        in_specs=[pl.BlockSpec((bm, bk), lambda i, k, j: (i, k)),
                  pl.BlockSpec((bk, bn), lambda i, k, j: (k, j)),
                  pl.BlockSpec((bm, bn), lambda i, k, j: (i, jcol(k, j))),
                  pl.BlockSpec((None, 1, bn), lambda i, k, j: (i // tiles_per_gate, 0, jcol(k, j)))],
        out_specs=pl.BlockSpec((bm, bn), lambda i, k, j: (i, jcol(k, j))),
        scratch_shapes=[pltpu.VMEM((bm, N), F32)],
        compiler_params=_cparams("parallel", "arbitrary", "arbitrary"),
        name=name,
    )(x, w, resid, gate.reshape(gate.shape[0], 1, N).astype(F32))


def _norm_kernel(x_ref, g_ref, *rest, modulate):
    if modulate:
        sc_ref, sh_ref, o_ref = rest
    else:
        (o_ref,) = rest
    x = x_ref[...]
    y = x * lax.rsqrt(jnp.mean(x * x, axis=-1, keepdims=True) + EPS) * g_ref[...]
    if modulate:
        y = y * (1.0 + sc_ref[...]) + sh_ref[...]
    o_ref[...] = y.astype(o_ref.dtype)


def _norm(x, g, scale=None, shift=None, *, out_dtype, name):
    B, L, D = x.shape
    rows = min(NORM_BLOCK_ROWS, L)
    assert L % rows == 0
    modulate = scale is not None
    operands = [x, g.reshape(1, D)]
    in_specs = [pl.BlockSpec((None, rows, D), lambda b, i: (b, i, 0)),
                pl.BlockSpec((1, D), lambda b, i: (0, 0))]
    if modulate:
        operands += [scale.reshape(B, 1, D), shift.reshape(B, 1, D)]
        in_specs += [pl.BlockSpec((None, 1, D), lambda b, i: (b, 0, 0))] * 2
    return pl.pallas_call(
        functools.partial(_norm_kernel, modulate=modulate),
        out_shape=jax.ShapeDtypeStruct((B, L, D), out_dtype),
        grid=(B, L // rows),
        in_specs=in_specs,
        out_specs=pl.BlockSpec((None, rows, D), lambda b, i: (b, i, 0)),
        compiler_params=_cparams("parallel", "parallel"),
        name=name,
    )(*operands)


def _silu(x):
    return x / (1.0 + jnp.exp(-x))


def _gated_head_norm(o, gate):
    o = o * lax.rsqrt(jnp.mean(o * o, axis=-1, keepdims=True) + EPS)
    return _silu(gate) * o


def _ret_kernel(lg_ref, q_ref, k_ref, v_ref, cos_ref, sin_ref, *rest, reverse, final, heads_per_step, hd):
    if final:
        op_ref, g_ref, o_ref, state_ref = rest
    else:
        o_ref, state_ref = rest
    C = q_ref.shape[0]
    half = hd // 2

    @pl.when(pl.program_id(2) == 0)
    def _():
        state_ref[...] = jnp.zeros_like(state_ref)

    cos, sin = cos_ref[...], sin_ref[...]
    row = lax.broadcasted_iota(jnp.int32, (C, C), 0)
    col = lax.broadcasted_iota(jnp.int32, (C, C), 1)
    ridx = lax.broadcasted_iota(jnp.int32, (C, 1), 0).astype(F32)
    if reverse:
        mask = col > row
        dist = (col - row).astype(F32)
        q_pow, k_pow = C - ridx, ridx
    else:
        mask = col <= row
        dist = (row - col).astype(F32)
        q_pow, k_pow = ridx + 1.0, C - 1.0 - ridx
    dist = jnp.where(mask, dist, 0.0)

    def rotate(x):
        x1, x2 = x[:, :half], x[:, half:]
        return jnp.concatenate([x1 * cos - x2 * sin, x1 * sin + x2 * cos], axis=1)

    for hh in range(heads_per_step):
        lg = lg_ref[pl.program_id(1) * heads_per_step + hh]
        sl = slice(hh * hd, (hh + 1) * hd)
        q = rotate(q_ref[:, sl])
        k = rotate(k_ref[:, sl]) * (hd ** -0.5)
        v = v_ref[:, sl].astype(BF16)
        intra = jnp.where(mask, jnp.exp(lg * dist), 0.0)
        s = lax.dot_general(q.astype(BF16), k.astype(BF16), (((1,), (1,)), ((), ())),
                            preferred_element_type=F32) * intra
        state = state_ref[hh]
        o = (jnp.dot(s.astype(BF16), v, preferred_element_type=F32)
             + jnp.dot((q * jnp.exp(lg * q_pow)).astype(BF16), state.astype(BF16),
                       preferred_element_type=F32))
        kd = (k * jnp.exp(lg * k_pow)).astype(BF16)
        state_ref[hh] = state * jnp.exp(lg * jnp.full((1, hd), C, F32)) + lax.dot_general(
            kd, v, (((0,), (0,)), ((), ())), preferred_element_type=F32)
        if final:
            o_ref[:, sl] = _gated_head_norm(op_ref[:, sl] + o, g_ref[:, sl]).astype(o_ref.dtype)
        else:
            o_ref[:, sl] = o


def _ret_pass(z, log_gamma, cos, sin, o_prev, *, reverse):
    B, L, D4 = z.shape
    D = D4 // 4
    hd = D // RET_HEADS
    hps = RET_HEADS_PER_STEP
    W = hps * hd
    C = min(RET_CHUNK, L)
    n = L // C
    nblk = D // W
    final = o_prev is not None
    cidx = (lambda c: n - 1 - c) if reverse else (lambda c: c)

    def zspec(part):
        return pl.BlockSpec((None, C, W), lambda b, h, c, lg: (b, cidx(c), part * nblk + h))

    tspec = pl.BlockSpec((C, hd // 2), lambda b, h, c, lg: (cidx(c), 0))
    ospec = pl.BlockSpec((None, C, W), lambda b, h, c, lg: (b, cidx(c), h))
    operands = [z, z, z, cos, sin]
    in_specs = [zspec(0), zspec(1), zspec(2), tspec, tspec]
    if final:
        operands += [o_prev, z]
        in_specs += [ospec, zspec(3)]
    kern = functools.partial(_ret_kernel, reverse=reverse, final=final, heads_per_step=hps, hd=hd)
    return pl.pallas_call(
        kern,
        out_shape=jax.ShapeDtypeStruct((B, L, D), BF16 if final else F32),
        grid_spec=pltpu.PrefetchScalarGridSpec(
            num_scalar_prefetch=1,
            grid=(B, nblk, n),
            in_specs=in_specs,
            out_specs=ospec,
            scratch_shapes=[pltpu.VMEM((hps, hd, hd), F32)]),
        compiler_params=_cparams("parallel", "parallel", "arbitrary"),
        name="ret_bwd" if reverse else "ret_fwd",
    )(log_gamma, *operands)


def _pair_split_perm(d, heads):
    hd = d // heads
    idx = np.arange(d).reshape(heads, hd)
    return np.concatenate([idx[:, 0::2], idx[:, 1::2]], axis=1).reshape(-1)


def _retention_mixer(h, x, g1, p):
    B, L, D = x.shape
    hd = D // RET_HEADS
    z = _matmul(h.reshape(B * L, D), p["w_in"], name="ret_in").reshape(B, L, 4 * D)
    pos = jnp.arange(L, dtype=F32)
    inv = 1.0 / (ROPE_BASE ** jnp.linspace(0.0, 1.0, hd // 2, dtype=F32))
    ang = pos[:, None] * inv[None, :]
    cos, sin = jnp.cos(ang), jnp.sin(ang)
    hidx = jnp.arange(RET_HEADS, dtype=F32)
    lg_fwd = jnp.log1p(-jnp.exp2(-5.0 - hidx))
    lg_bwd = jnp.log1p(-jnp.exp2(-5.0 - hidx[::-1]))
    o_fwd = _ret_pass(z, lg_fwd, cos, sin, None, reverse=False)
    y = _ret_pass(z, lg_bwd, cos, sin, o_fwd, reverse=True)
    out = _matmul(y.reshape(B * L, D), p["w_out"], resid=x.reshape(B * L, D), gate=g1,
                  rows_per_gate=L, name="ret_out", **MM_OUT)
    return out.reshape(B, L, D)


def _gla_kernel(q_ref, k_ref, v_ref, t_ref, w2_ref, gb_ref, *rest, reverse, final, sub):
    if final:
        op_ref, r_ref, o_ref, state_ref, a_ref = rest
    else:
        o_ref, state_ref, a_ref = rest
    C, dk = q_ref.shape

    @pl.when(pl.program_id(2) == 0)
    def _():
        state_ref[...] = jnp.zeros_like(state_ref)

    zg = jnp.dot(t_ref[...].astype(BF16), w2_ref[...].astype(BF16), preferred_element_type=F32) + gb_ref[...]
    log_a = (jnp.minimum(zg, 0.0) - jnp.log1p(jnp.exp(-jnp.abs(zg)))) * (1.0 / GLA_TAU)

    row = lax.broadcasted_iota(jnp.int32, (C, C), 0)
    col = lax.broadcasted_iota(jnp.int32, (C, C), 1)
    tri = ((col >= row) if reverse else (col <= row)).astype(BF16)
    hi = log_a.astype(BF16)
    lo = (log_a - hi.astype(F32)).astype(BF16)
    b = (jnp.dot(tri, hi, preferred_element_type=F32) + jnp.dot(tri, lo, preferred_element_type=F32))
    b_tot = jnp.sum(log_a, axis=0, keepdims=True)

    q = q_ref[...] * (dk ** -0.5)
    k = k_ref[...]
    v = v_ref[...].astype(BF16)

    for blk in range(C // sub):
        rows = slice(blk * sub, (blk + 1) * sub)
        ref = b[blk * sub + sub // 2:blk * sub + sub // 2 + 1, :]
        qh = q[rows] * jnp.exp(jnp.minimum(b[rows] - ref, GLA_EXP_CLAMP))
        kh = k * jnp.exp(jnp.minimum(ref - b, GLA_EXP_CLAMP))
        s = lax.dot_general(qh.astype(BF16), kh.astype(BF16), (((1,), (1,)), ((), ())),
                            preferred_element_type=F32)
        r_i = lax.broadcasted_iota(jnp.int32, (sub, C), 0) + blk * sub
        c_i = lax.broadcasted_iota(jnp.int32, (sub, C), 1)
        keep = (c_i > r_i) if reverse else (c_i <= r_i)
        a_ref[rows, :] = jnp.where(keep, s, 0.0).astype(BF16)

    state = state_ref[...]
    o = (jnp.dot(a_ref[...], v, preferred_element_type=F32)
         + lax.dot_general((q * jnp.exp(b)).astype(BF16), state.astype(BF16), (((1,), (1,)), ((), ())),
                           preferred_element_type=F32))
    kd = (k * jnp.exp(b_tot - b)).astype(BF16)
    state_ref[...] = state * jnp.exp(b_tot) + lax.dot_general(
        v, kd, (((0,), (0,)), ((), ())), preferred_element_type=F32)
    if final:
        o_ref[...] = _gated_head_norm(op_ref[...] + o, r_ref[...]).astype(o_ref.dtype)
    else:
        o_ref[...] = o


def _gla_pass(z, t, w2, gb, o_prev, *, direction):
    B, L, D3 = z.shape
    D = D3 // 3
    dk = D // 2 // GLA_HEADS
    dv = D // GLA_HEADS
    C = min(GLA_CHUNK, L)
    n = L // C
    reverse = direction == 1
    final = o_prev is not None
    cidx = (lambda c: n - 1 - c) if reverse else (lambda c: c)
    H = GLA_HEADS

    qspec = pl.BlockSpec((None, C, dk), lambda b, h, c: (b, cidx(c), h))
    kspec = pl.BlockSpec((None, C, dk), lambda b, h, c: (b, cidx(c), H + h))
    vspec = pl.BlockSpec((None, C, dv), lambda b, h, c: (b, cidx(c), H + h))
    rspec = pl.BlockSpec((None, C, dv), lambda b, h, c: (b, cidx(c), 2 * H + h))
    ospec = pl.BlockSpec((None, C, dv), lambda b, h, c: (b, cidx(c), h))
    tspec = pl.BlockSpec((None, C, t.shape[-1]), lambda b, h, c: (b, cidx(c), 0))
    w2spec = pl.BlockSpec((None, t.shape[-1], dk), lambda b, h, c: (direction, 0, h))
    gbspec = pl.BlockSpec((None, 1, dk), lambda b, h, c: (direction, 0, h))
    operands = [z, z, z, t, w2, gb]
    in_specs = [qspec, kspec, vspec, tspec, w2spec, gbspec]
    if final:
        operands += [o_prev, z]
        in_specs += [ospec, rspec]
    kern = functools.partial(_gla_kernel, reverse=reverse, final=final, sub=min(GLA_SUBBLOCK, C))
    return pl.pallas_call(
        kern,
        out_shape=jax.ShapeDtypeStruct((B, L, D), BF16 if final else F32),
        grid=(B, H, n),
        in_specs=in_specs,
        out_specs=ospec,
        scratch_shapes=[pltpu.VMEM((dv, dk), F32), pltpu.VMEM((C, C), BF16)],
        compiler_params=_cparams("parallel", "parallel", "arbitrary"),
        name="gla_bwd" if reverse else "gla_fwd",
    )(*operands)


def _gla_mixer(h, x, g1, p):
    B, L, D = x.shape
    h2 = h.reshape(B * L, D)
    z = _matmul(h2, p["w_in"], name="gla_in").reshape(B, L, 3 * D)
    t = _matmul(h2, p["gate_w1"], bn=V7X_LANES, name="gla_gate").reshape(B, L, V7X_LANES)
    o_fwd = _gla_pass(z, t, p["gate_w2"], p["gate_b"], None, direction=0)
    y = _gla_pass(z, t, p["gate_w2"], p["gate_b"], o_fwd, direction=1)
    out = _matmul(y.reshape(B * L, D), p["w_out"], resid=x.reshape(B * L, D), gate=g1,
                  rows_per_gate=L, name="gla_out", **MM_OUT)
    return out.reshape(B, L, D)


def _conv_gate_kernel(*refs, rows):
    groups = [refs[5 * g:5 * g + 5] for g in range(3)]
    x0_out, vv_out, pad_ref = refs[15:]
    i = pl.program_id(1)
    first = i == 0
    last = i == pl.num_programs(1) - 1

    def conv(cur_ref, prev_ref, next_ref, w_ref, b_ref):
        pad_ref[pl.ds(8, rows), :] = cur_ref[...]
        pad_ref[pl.ds(0, 8), :] = jnp.where(first, 0.0, prev_ref[...])
        pad_ref[pl.ds(rows + 8, 8), :] = jnp.where(last, 0.0, next_ref[...])
        w = w_ref[...]
        return (pad_ref[pl.ds(7, rows), :] * w[0:1] + cur_ref[...] * w[1:2]
                + pad_ref[pl.ds(9, rows), :] * w[2:3] + b_ref[...])

    x0_out[...] = conv(*groups[0])
    x1 = conv(*groups[1])
    vv_out[...] = conv(*groups[2]) * x1


def _conv_gate(z, conv_w, conv_b):
    B, L, D3 = z.shape
    D = D3 // 3
    rows, ct = min(512, L), 512
    nct = D // ct
    nhalo = L // 8
    operands, in_specs = [], []
    for g in range(3):
        operands += [z, z, z, conv_w, conv_b.reshape(1, D3)]
        in_specs += [
            pl.BlockSpec((None, rows, ct), lambda b, i, j, g=g: (b, i, g * nct + j)),
            pl.BlockSpec((None, 8, ct), lambda b, i, j, g=g: (b, jnp.maximum(i * (rows // 8) - 1, 0), g * nct + j)),
            pl.BlockSpec((None, 8, ct), lambda b, i, j, g=g: (b, jnp.minimum((i + 1) * (rows // 8), nhalo - 1), g * nct + j)),
            pl.BlockSpec((3, ct), lambda b, i, j, g=g: (0, g * nct + j)),
            pl.BlockSpec((1, ct), lambda b, i, j, g=g: (0, g * nct + j)),
        ]
    ospec = pl.BlockSpec((None, rows, ct), lambda b, i, j: (b, i, j))
    return pl.pallas_call(
        functools.partial(_conv_gate_kernel, rows=rows),
        out_shape=(jax.ShapeDtypeStruct((B, L, D), F32), jax.ShapeDtypeStruct((B, L, D), F32)),
        grid=(B, L // rows, nct),
        in_specs=in_specs,
        out_specs=(ospec, ospec),
        scratch_shapes=[pltpu.VMEM((rows + 16, ct), F32)],
        compiler_params=_cparams("parallel", "parallel", "parallel"),
        name="hy_conv_gate",
    )(*operands)


FFT_GROUP = 16
FFT_K1_GROUP = 8
FFT_LANES = 128


def _stage_a(x_parts, g_ref, z_ref, stage_ref, step, *, n1, ng, mirror_second=False):
    xt = [jnp.swapaxes(x, 0, 1) for x in x_parts]
    for j in range(FFT_GROUP):
        x = jnp.concatenate([xt[0][j], xt[1][FFT_GROUP - 1 - j if mirror_second else j]], axis=0)
        stage_ref[j] = jnp.dot(g_ref[j], x.astype(BF16), preferred_element_type=F32)
    a = jnp.swapaxes(stage_ref[...], 0, 1)
    z_ref[:, step] = a[:n1]
    z_ref[:, ng + step] = a[n1:]


def _fft_conv_kernel(x_ref, x0_ref, g_ref, f_ref, fi_ref, kf_ref, gi_ref, skip_ref, o_ref, z_ref, sa_ref, sc_ref,
                     *, sa, sb, n1, n2, h1):
    step = pl.program_id(2)
    ng = n2 // FFT_GROUP

    @pl.when(step < sa)
    def _():
        _stage_a([x_ref[0], x_ref[1]], g_ref, z_ref, sa_ref, step, n1=n1, ng=ng)

    @pl.when(jnp.logical_and(step >= sa, step < sa + sb))
    def _():
        for j in range(FFT_K1_GROUP):
            k1 = (step - sa) * FFT_K1_GROUP + j
            zk = z_ref[k1].reshape(2 * n2, z_ref.shape[-1]).astype(BF16)
            y = jnp.dot(f_ref[...], zk, preferred_element_type=F32)
            yr, yi = y[:n2], y[n2:]
            kr, ki = kf_ref[j, :n2, :], kf_ref[j, n2:, :]
            p = jnp.concatenate([yr * kr - yi * ki, yr * ki + yi * kr], axis=0).astype(BF16)
            z_ref[k1] = jnp.dot(fi_ref[...], p, preferred_element_type=F32).reshape(z_ref.shape[1:])

    @pl.when(step >= sa + sb)
    def _():
        g = step - sa - sb
        zr = jnp.swapaxes(z_ref[:, g], 0, 1)
        zi = jnp.swapaxes(z_ref[:, ng + g], 0, 1)
        for j in range(FFT_GROUP):
            bz = jnp.concatenate([zr[j], zi[j]], axis=0).astype(BF16)
            sc_ref[j] = jnp.dot(gi_ref[j], bz, preferred_element_type=F32)
        y = jnp.swapaxes(sc_ref[...], 0, 1).reshape(x_ref.shape)
        o_ref[...] = ((y + skip_ref[...] * x_ref[...]) * x0_ref[...]).astype(o_ref.dtype)


def _dft_tables(L):
    N = 2 * L
    n1 = n2 = int(round(math.sqrt(N)))
    assert n1 * n2 == N and n2 % FFT_GROUP == 0 and n1 % FFT_K1_GROUP == 0
    h1 = n1 // 2
    k1 = jnp.arange(n1, dtype=jnp.int32)
    m = (jnp.arange(n2, dtype=jnp.int32)[:, None, None] * k1[None, :, None]
         + n2 * k1[None, :, None] * jnp.arange(h1, dtype=jnp.int32)[None, None, :]) % N
    ang = (2.0 * math.pi / N) * m.astype(F32)
    c, s = jnp.cos(ang), jnp.sin(ang)
    block = lambda re, im: jnp.concatenate([jnp.concatenate([re, -im], -1), jnp.concatenate([im, re], -1)], -2)
    g = block(c, -s).astype(BF16)
    ct_, st_ = jnp.swapaxes(c, 1, 2) / N, jnp.swapaxes(s, 1, 2) / N
    gi = block(ct_, st_).astype(BF16)
    idx = jnp.arange(n2, dtype=jnp.int32)
    ang2 = (2.0 * math.pi / n2) * ((idx[:, None] * idx[None, :]) % n2).astype(F32)
    c2, s2 = jnp.cos(ang2), jnp.sin(ang2)
    return dict(n1=n1, n2=n2, h1=h1, g=g, gi=gi, f=block(c2, -s2).astype(BF16), fi=block(c2, s2).astype(BF16))


def _filter_spectrum_kernel(hf_ref, hb_ref, g_ref, f_ref, sum_ref, o_ref, z_ref, sa_ref, *, sa, n1, n2):
    step = pl.program_id(1)
    ng = n2 // FFT_GROUP

    @pl.when(step < sa)
    def _():
        _stage_a([hf_ref[...], hb_ref[...]], g_ref, z_ref, sa_ref, step, n1=n1, ng=ng, mirror_second=True)

    @pl.when(step >= sa)
    def _():
        scale = 1.0 / (sum_ref[...] + EPS)
        for j in range(FFT_K1_GROUP):
            k1 = (step - sa) * FFT_K1_GROUP + j
            zk = z_ref[k1].reshape(2 * n2, z_ref.shape[-1]).astype(BF16)
            o_ref[j] = jnp.dot(f_ref[...], zk, preferred_element_type=F32) * scale


def _filter_tables(tab):
    n1, n2, h1 = tab["n1"], tab["n2"], tab["h1"]
    N = n1 * n2
    k1 = jnp.arange(n1, dtype=jnp.int32)[None, :, None]
    i2 = jnp.arange(n2, dtype=jnp.int32)[:, None, None]
    a = jnp.arange(h1, dtype=jnp.int32)[None, None, :]
    rows = jnp.concatenate([a, n1 - 1 - a], axis=-1)
    m = (i2 * k1 + n2 * k1 * rows) % N
    ang = (2.0 * math.pi / N) * m.astype(F32)
    col = jnp.arange(2 * h1, dtype=jnp.int32)[None, None, :]
    live = jnp.logical_not(jnp.logical_and(i2 == 0, col == 2 * h1 - 1))
    re = jnp.where(live, jnp.cos(ang), 0.0)
    im = jnp.where(live, -jnp.sin(ang), 0.0)
    return jnp.concatenate([re, im], axis=1).astype(BF16)


def _filter_spectrum(hf, hb, abs_sum, tab):
    L, D = hf.shape
    n1, n2, h1 = tab["n1"], tab["n2"], tab["h1"]
    ct = FFT_LANES
    sa = n2 // FFT_GROUP
    sb = n1 // FFT_K1_GROUP
    kern = functools.partial(_filter_spectrum_kernel, sa=sa, n1=n1, n2=n2)
    return pl.pallas_call(
        kern,
        out_shape=jax.ShapeDtypeStruct((n1, 2 * n2, D), F32),
        grid=(D // ct, sa + sb),
        in_specs=[
            pl.BlockSpec((h1, FFT_GROUP, ct), lambda t, s: (0, jnp.minimum(s, sa - 1), t)),
            pl.BlockSpec((h1, FFT_GROUP, ct), lambda t, s: (0, sa - 1 - jnp.minimum(s, sa - 1), t)),
            pl.BlockSpec((FFT_GROUP, 2 * n1, 2 * h1), lambda t, s: (jnp.minimum(s, sa - 1), 0, 0)),
            pl.BlockSpec((2 * n2, 2 * n2), lambda t, s: (0, 0)),
            pl.BlockSpec((1, ct), lambda t, s: (0, t)),
        ],
        out_specs=pl.BlockSpec((FFT_K1_GROUP, 2 * n2, ct), lambda t, s: (jnp.maximum(s - sa, 0), 0, t)),
        scratch_shapes=[pltpu.VMEM((n1, 2 * n2 // FFT_GROUP, FFT_GROUP, ct), F32),
                        pltpu.VMEM((FFT_GROUP, 2 * n1, ct), F32)],
        compiler_params=_cparams("parallel", "arbitrary"),
        name="hy_filter_spectrum",
    )(hf.reshape(h1, n2, D), hb.reshape(h1, n2, D), tab["gf"], tab["f"], abs_sum)


def _fft_conv(vv, x0, kf, skip, tab):
    B, L, D = vv.shape
    n1, n2, h1 = tab["n1"], tab["n2"], tab["h1"]
    assert B % 2 == 0 and L == h1 * n2
    ct = FFT_LANES
    sa = n2 // FFT_GROUP
    sb = n1 // FFT_K1_GROUP
    sc = sa
    shape5 = (B // 2, 2, h1, n2, D)
    a_idx = lambda s: jnp.where(s < sa, s, jnp.where(s >= sa + sb, s - sa - sb, sa - 1))
    c_idx = lambda s: jnp.clip(s - sa - sb, 0, sc - 1)
    blk = (None, 2, h1, FFT_GROUP, ct)
    in_specs = [
        pl.BlockSpec(blk, lambda p, t, s: (p, 0, 0, a_idx(s), t)),
        pl.BlockSpec(blk, lambda p, t, s: (p, 0, 0, c_idx(s), t)),
        pl.BlockSpec((FFT_GROUP, 2 * n1, 2 * h1), lambda p, t, s: (jnp.minimum(s, sa - 1), 0, 0)),
        pl.BlockSpec((2 * n2, 2 * n2), lambda p, t, s: (0, 0)),
        pl.BlockSpec((2 * n2, 2 * n2), lambda p, t, s: (0, 0)),
        pl.BlockSpec((FFT_K1_GROUP, 2 * n2, ct), lambda p, t, s: (jnp.clip(s - sa, 0, sb - 1), 0, t)),
        pl.BlockSpec((FFT_GROUP, 2 * h1, 2 * n1), lambda p, t, s: (c_idx(s), 0, 0)),
        pl.BlockSpec((1, ct), lambda p, t, s: (0, t)),
    ]
    kern = functools.partial(_fft_conv_kernel, sa=sa, sb=sb, n1=n1, n2=n2, h1=h1)
    y = pl.pallas_call(
        kern,
        out_shape=jax.ShapeDtypeStruct(shape5, BF16),
        grid=(B // 2, D // ct, sa + sb + sc),
        in_specs=in_specs,
        out_specs=pl.BlockSpec(blk, lambda p, t, s: (p, 0, 0, c_idx(s), t)),
        scratch_shapes=[pltpu.VMEM((n1, 2 * n2 // FFT_GROUP, FFT_GROUP, ct), F32),
                        pltpu.VMEM((FFT_GROUP, 2 * n1, ct), F32),
                        pltpu.VMEM((FFT_GROUP, 2 * h1, ct), F32)],
        compiler_params=_cparams("parallel", "parallel", "arbitrary"),
        name="hy_fft_conv",
    )(vv.reshape(shape5), x0.reshape(shape5), tab["g"], tab["f"], tab["fi"], kf, tab["gi"],
      skip.reshape(1, D))
    return y.reshape(B, L, D)


def _filter_taps_kernel(feat_ref, t_ref, wf_ref, wb_ref, dl_ref, hf_ref, hb_ref, sum_ref, *, rows, total_rows):
    i = pl.program_id(1)
    window = jnp.exp(-t_ref[...] * dl_ref[...]) + HY_SHIFT
    feat = feat_ref[...]
    hf = jnp.dot(feat, wf_ref[...], preferred_element_type=F32, precision=lax.Precision.HIGHEST) * window
    hb = jnp.dot(feat, wb_ref[...], preferred_element_type=F32, precision=lax.Precision.HIGHEST) * window
    hf_ref[...] = hf
    hb_ref[...] = hb
    row = lax.broadcasted_iota(jnp.int32, hb.shape, 0) + i * rows
    part = (jnp.sum(jnp.abs(hf), axis=0, keepdims=True)
            + jnp.sum(jnp.where(row < total_rows - 1, jnp.abs(hb), 0.0), axis=0, keepdims=True))

    @pl.when(i == 0)
    def _():
        sum_ref[...] = part

    @pl.when(i > 0)
    def _():
        sum_ref[...] += part


def _filter_taps(L, w1, b1, w2, b2, w3, b3, w_out, freq):
    hp = lax.Precision.HIGHEST
    D = w_out.shape[-1] // 2
    t = jnp.linspace(0.0, 1.0, L, dtype=F32)[:, None]
    w = (2.0 * math.pi / L) * jnp.arange(L, dtype=F32)[:, None]
    bands = jnp.linspace(1e-4, HY_BANDS - 1, HY_BANDS, dtype=F32)[None, :]
    z = jnp.concatenate([t, jnp.cos(bands * w), -jnp.sin(bands * w)], axis=-1)
    h = jnp.sin(freq * (jnp.dot(z, w1, precision=hp) + b1))
    h = jnp.sin(freq * (jnp.dot(h, w2, precision=hp) + b2))
    h = jnp.sin(freq * (jnp.dot(h, w3, precision=hp) + b3))
    max_decay = math.log(HY_TARGET) / HY_FAST_DECAY
    min_decay = math.log(HY_TARGET) / HY_SLOW_DECAY
    deltas = jnp.abs(jnp.linspace(min_decay, max_decay, D, dtype=F32))[None, :]
    rows, ct = min(512, L), 512
    width = h.shape[-1]
    nct = D // ct
    tap_spec = pl.BlockSpec((rows, ct), lambda j, i: (i, j))
    return pl.pallas_call(
        functools.partial(_filter_taps_kernel, rows=rows, total_rows=L),
        out_shape=(jax.ShapeDtypeStruct((L, D), F32), jax.ShapeDtypeStruct((L, D), F32),
                   jax.ShapeDtypeStruct((1, D), F32)),
        grid=(nct, L // rows),
        in_specs=[pl.BlockSpec((rows, width), lambda j, i: (i, 0)),
                  pl.BlockSpec((rows, 1), lambda j, i: (i, 0)),
                  pl.BlockSpec((width, ct), lambda j, i: (0, j)),
                  pl.BlockSpec((width, ct), lambda j, i: (0, nct + j)),
                  pl.BlockSpec((1, ct), lambda j, i: (0, j))],
        out_specs=(tap_spec, tap_spec, pl.BlockSpec((1, ct), lambda j, i: (0, j))),
        compiler_params=_cparams("parallel", "arbitrary"),
        name="hy_filter_taps",
    )(h, t, w_out, w_out, deltas)


def _hyena_mixer(h, x, g1, p, tab):
    B, L, D = x.shape
    z = _matmul(h.reshape(B * L, D), p["w_in"], bias=p["b_in"], name="hy_in").reshape(B, L, 3 * D)
    x0, vv = _conv_gate(z, p["conv_w"], p["conv_b"])
    hf, hb, abs_sum = _filter_taps(L, p["f_w1"], p["f_b1"], p["f_w2"], p["f_b2"], p["f_w3"], p["f_b3"],
                                   p["f_wout"], p["freq"])
    y = _fft_conv(vv, x0, _filter_spectrum(hf, hb, abs_sum, tab), p["skip"], tab)
    out = _matmul(y.reshape(B * L, D), p["w_out"], bias=p["b_out"],
                  resid=x.reshape(B * L, D), gate=g1, rows_per_gate=L, name="hy_out", **MM_OUT)
    return out.reshape(B, L, D)


def _trunk(x, mod, hy, ret, gla, norm_g, mlp_w1, mlp_w2, final_g):
    B, L, D = x.shape
    tab = _dft_tables(L)
    tab["gf"] = _filter_tables(tab)
    for i in range(DEPTH):
        sh1, sc1, g1, sh2, sc2, g2 = jnp.split(mod[i], 6, axis=-1)
        h = _norm(x, norm_g[i, 0], sc1, sh1, out_dtype=BF16, name="norm_mix")
        kind, slot = i % N_MIXERS, i // N_MIXERS
        if kind == 0:
            x = _hyena_mixer(h, x, g1, {n: v[slot] for n, v in hy.items()}, tab)
        elif kind == 1:
            x = _retention_mixer(h, x, g1, {n: v[slot] for n, v in ret.items()})
        else:
            x = _gla_mixer(h, x, g1, {n: v[slot] for n, v in gla.items()})
        h = _norm(x, norm_g[i, 1], sc2, sh2, out_dtype=BF16, name="norm_mlp")
        a = _matmul(h.reshape(B * L, D), mlp_w1[i], act="relu2", out_dtype=BF16, name="mlp_up")
        x = _matmul_deep(a, mlp_w2[i], x.reshape(B * L, D), g2, L, name="mlp_down").reshape(B, L, D)
    return _norm(x, final_g, out_dtype=F32, name="norm_final")


def kernel(x_prompt, x_sample, c_prompt, c_sample, hy_w_in, hy_b_in, hy_conv_w, hy_conv_b, hy_f_w1, hy_f_b1, hy_f_w2, hy_f_b2, hy_f_w3, hy_f_b3, hy_f_wout, hy_freq, hy_skip, hy_w_out, hy_b_out, ret_w_in, ret_w_out, gla_w_in, gla_gate_w1, gla_gate_w2, gla_gate_b, gla_w_out, norm_g, ada_w, ada_b, mlp_w1, mlp_w2, final_g):
    D = x_prompt.shape[-1]
    bf = lambda w: w.astype(BF16)
    hy = dict(w_in=bf(hy_w_in), b_in=hy_b_in, conv_w=hy_conv_w, conv_b=hy_conv_b,
              f_w1=hy_f_w1, f_b1=hy_f_b1, f_w2=hy_f_w2, f_b2=hy_f_b2, f_w3=hy_f_w3, f_b3=hy_f_b3,
              f_wout=hy_f_wout, freq=hy_freq, skip=hy_skip, w_out=bf(hy_w_out), b_out=hy_b_out)

    perm = _pair_split_perm(D, RET_HEADS)
    cols = np.concatenate([perm, D + perm, np.arange(2 * D, 4 * D)])
    ret = dict(w_in=bf(ret_w_in[:, :, cols]), w_out=bf(ret_w_out))

    n_gla, _, _, rank = gla_gate_w1.shape
    w1 = jnp.concatenate([gla_gate_w1[:, 0], gla_gate_w1[:, 1],
                          jnp.zeros((n_gla, D, V7X_LANES - 2 * rank), F32)], axis=-1)
    w2 = jnp.zeros((n_gla, 2, V7X_LANES, gla_gate_w2.shape[-1]), F32)
    w2 = w2.at[:, 0, :rank].set(gla_gate_w2[:, 0]).at[:, 1, rank:2 * rank].set(gla_gate_w2[:, 1])
    gla = dict(w_in=bf(gla_w_in), gate_w1=bf(w1), gate_w2=w2, gate_b=gla_gate_b[:, :, None, :],
               w_out=bf(gla_w_out))
    mw1, mw2 = bf(mlp_w1), bf(mlp_w2)

    nb_p, nb_s = c_prompt.shape[0], c_sample.shape[0]
    cs = jax.nn.silu(jnp.concatenate([c_prompt, c_sample], axis=0))
    rows = -(-cs.shape[0] // 16) * 16
    cs = jnp.pad(cs, ((0, rows - cs.shape[0]), (0, 0))).astype(BF16)
    mod = jnp.stack([_matmul(cs, ada_w[i], bias=ada_b[i], bn=512, name="ada") for i in range(DEPTH)])
    mod_p, mod_s = mod[:, :nb_p], mod[:, nb_p:nb_p + nb_s]

    y_prompt = _trunk(x_prompt, mod_p, hy, ret, gla, norm_g, mw1, mw2, final_g)
    y_sample = _trunk(x_sample, mod_s, hy, ret, gla, norm_g, mw1, mw2, final_g)
    return (y_prompt, y_sample)
```

```python
import functools
import math

import jax
import jax.numpy as jnp
import numpy as np
from jax import lax
from jax.experimental import pallas as pl
from jax.experimental.pallas import tpu as pltpu

D_MODEL = 4096
DEPTH = 4
N_MIXERS = 3
EPS = 1e-6
HY_BANDS = 16
HY_FAST_DECAY = 0.3
HY_SLOW_DECAY = 1.5
HY_TARGET = 1e-2
HY_SHIFT = 0.05
RET_HEADS = 16
ROPE_BASE = 10000.0
GLA_HEADS = 4
GLA_GATE_RANK = 16
GLA_TAU = 16.0

F32 = jnp.float32
BF16 = jnp.bfloat16

V7X_VMEM_LIMIT_BYTES = 56 * 1024 * 1024
V7X_LANES = 128
MM_BLOCK_M = 1024
MM_BLOCK_N = 1024
MM_BLOCK_K = 4096
MM_OUT = dict(bn=512)
NORM_BLOCK_ROWS = 512
RET_CHUNK = 256
RET_HEADS_PER_STEP = 4
GLA_CHUNK = 128
GLA_SUBBLOCK = 32
GLA_EXP_CLAMP = 80.0


def _cparams(*sem):
    return pltpu.CompilerParams(dimension_semantics=sem, vmem_limit_bytes=V7X_VMEM_LIMIT_BYTES)


def _mm_kernel(*refs, nk, has_bias, has_resid, act):
    it = iter(refs)
    x_ref, w_ref = next(it), next(it)
    b_ref = next(it) if has_bias else None
    r_ref = next(it) if has_resid else None
    g_ref = next(it) if has_resid else None
    o_ref = next(it)
    acc_ref = next(it) if nk > 1 else None

    part = jnp.dot(x_ref[...].astype(BF16), w_ref[...].astype(BF16), preferred_element_type=F32)

    def finish(acc):
        if has_bias:
            acc = acc + b_ref[...]
        if act == "relu2":
            acc = jnp.maximum(acc, 0.0)
            acc = acc * acc
        if has_resid:
            acc = r_ref[...] + g_ref[...] * acc
        o_ref[...] = acc.astype(o_ref.dtype)

    if nk == 1:
        finish(part)
    else:
        k = pl.program_id(2)

        @pl.when(k == 0)
        def _():
            acc_ref[...] = part

        @pl.when(jnp.logical_and(k > 0, k < nk - 1))
        def _():
            acc_ref[...] += part

        @pl.when(k == nk - 1)
        def _():
            finish(acc_ref[...] + part)


def _matmul(x, w, *, layer=None, bias=None, act=None, resid=None, gate=None, rows_per_gate=None,
            out_dtype=F32, bm=MM_BLOCK_M, bn=MM_BLOCK_N, bk=MM_BLOCK_K, name="matmul"):
    M, K = x.shape
    K2, N = w.shape[-2:]
    assert K == K2
    bm, bn, bk = min(bm, M, rows_per_gate or M), min(bn, N), min(bk, K)
    assert M % bm == 0 and N % bn == 0 and K % bk == 0
    nk = K // bk
    has_bias = bias is not None
    has_resid = resid is not None

    operands = [x, w]
    in_specs = [pl.BlockSpec((bm, bk), lambda i, j, k: (i, k)),
                pl.BlockSpec((bk, bn), lambda i, j, k: (k, j)) if layer is None else
                pl.BlockSpec((None, bk, bn), lambda i, j, k: (layer, k, j))]
    if has_bias:
        operands.append(bias.reshape(1, N).astype(F32))
        in_specs.append(pl.BlockSpec((1, bn), lambda i, j, k: (0, j)))
    if has_resid:
        assert rows_per_gate % bm == 0
        tiles_per_gate = rows_per_gate // bm
        operands.append(resid)
        in_specs.append(pl.BlockSpec((bm, bn), lambda i, j, k: (i, j)))
        operands.append(gate.reshape(gate.shape[0], 1, N).astype(F32))
        in_specs.append(pl.BlockSpec((None, 1, bn), lambda i, j, k: (i // tiles_per_gate, 0, j)))

    scratch = [pltpu.VMEM((bm, bn), F32)] if nk > 1 else []
    kern = functools.partial(_mm_kernel, nk=nk, has_bias=has_bias, has_resid=has_resid, act=act)
    return pl.pallas_call(
        kern,
        out_shape=jax.ShapeDtypeStruct((M, N), out_dtype),
        grid=(M // bm, N // bn, nk),
        in_specs=in_specs,
        out_specs=pl.BlockSpec((bm, bn), lambda i, j, k: (i, j)),
        scratch_shapes=scratch,
        compiler_params=_cparams("parallel", "parallel", "arbitrary"),
        name=name,
    )(*operands)


def _mm_deep_kernel(x_ref, w_ref, r_ref, g_ref, o_ref, acc_ref, *, nk, bn):
    k = pl.program_id(1)
    cols = pl.ds(pl.multiple_of(pl.program_id(2) * bn, bn), bn)
    part = jnp.dot(x_ref[...], w_ref[...], preferred_element_type=F32)

    @pl.when(k == 0)
    def _():
        acc_ref[:, cols] = part

    @pl.when(jnp.logical_and(k > 0, k < nk - 1))
    def _():
        acc_ref[:, cols] += part

    @pl.when(k == nk - 1)
    def _():
        o_ref[...] = r_ref[...] + g_ref[...] * (acc_ref[:, cols] + part)


def _matmul_deep(x, w, layer, resid, gate, rows_per_gate, *, bm=MM_BLOCK_M, bn=MM_BLOCK_N, bk=2048, name):
    M, K = x.shape
    N = w.shape[-1]
    bm = min(bm, M, rows_per_gate)
    assert M % bm == 0 and N % bn == 0 and K % bk == 0 and rows_per_gate % bm == 0
    nk = K // bk
    assert nk >= 2
    tiles_per_gate = rows_per_gate // bm
    jcol = lambda k, j: jnp.where(k == nk - 1, j, 0)
    return pl.pallas_call(
        functools.partial(_mm_deep_kernel, nk=nk, bn=bn),
        out_shape=jax.ShapeDtypeStruct((M, N), F32),
        grid=(M // bm, nk, N // bn),
        in_specs=[pl.BlockSpec((bm, bk), lambda i, k, j: (i, k)),
                  pl.BlockSpec((None, bk, bn), lambda i, k, j: (layer, k, j)),
                  pl.BlockSpec((bm, bn), lambda i, k, j: (i, jcol(k, j))),
                  pl.BlockSpec((None, 1, bn), lambda i, k, j: (i // tiles_per_gate, 0, jcol(k, j)))],
        out_specs=pl.BlockSpec((bm, bn), lambda i, k, j: (i, jcol(k, j))),
        scratch_shapes=[pltpu.VMEM((bm, N), F32)],
        compiler_params=_cparams("parallel", "arbitrary", "arbitrary"),
        name=name,
    )(x, w, resid, gate.reshape(gate.shape[0], 1, N).astype(F32))


def _norm_kernel(x_ref, g_ref, *rest, modulate):
    if modulate:
        sc_ref, sh_ref, o_ref = rest
    else:
        (o_ref,) = rest
    x = x_ref[...]
    y = x * lax.rsqrt(jnp.mean(x * x, axis=-1, keepdims=True) + EPS) * g_ref[...]
    if modulate:
        y = y * (1.0 + sc_ref[...]) + sh_ref[...]
    o_ref[...] = y.astype(o_ref.dtype)


def _norm(x, g, scale=None, shift=None, *, out_dtype, name):
    B, L, D = x.shape
    rows = min(NORM_BLOCK_ROWS, L)
    assert L % rows == 0
    modulate = scale is not None
    operands = [x, g.reshape(1, D)]
    in_specs = [pl.BlockSpec((None, rows, D), lambda b, i: (b, i, 0)),
                pl.BlockSpec((1, D), lambda b, i: (0, 0))]
    if modulate:
        operands += [scale.reshape(B, 1, D), shift.reshape(B, 1, D)]
        in_specs += [pl.BlockSpec((None, 1, D), lambda b, i: (b, 0, 0))] * 2
    return pl.pallas_call(
        functools.partial(_norm_kernel, modulate=modulate),
        out_shape=jax.ShapeDtypeStruct((B, L, D), out_dtype),
        grid=(B, L // rows),
        in_specs=in_specs,
        out_specs=pl.BlockSpec((None, rows, D), lambda b, i: (b, i, 0)),
        compiler_params=_cparams("parallel", "parallel"),
        name=name,
    )(*operands)


def _silu(x):
    return x / (1.0 + jnp.exp(-x))


def _gated_head_norm(o, gate):
    o = o * lax.rsqrt(jnp.mean(o * o, axis=-1, keepdims=True) + EPS)
    return _silu(gate) * o


def _ret_kernel(lg_ref, q_ref, k_ref, v_ref, cos_ref, sin_ref, *rest, reverse, final, heads_per_step, hd):
    if final:
        op_ref, g_ref, o_ref, state_ref = rest
    else:
        o_ref, state_ref = rest
    C = q_ref.shape[0]
    half = hd // 2

    @pl.when(pl.program_id(2) == 0)
    def _():
        state_ref[...] = jnp.zeros_like(state_ref)

    cos, sin = cos_ref[...], sin_ref[...]
    row = lax.broadcasted_iota(jnp.int32, (C, C), 0)
    col = lax.broadcasted_iota(jnp.int32, (C, C), 1)
    ridx = lax.broadcasted_iota(jnp.int32, (C, 1), 0).astype(F32)
    if reverse:
        mask = col > row
        dist = (col - row).astype(F32)
        q_pow, k_pow = C - ridx, ridx
    else:
        mask = col <= row
        dist = (row - col).astype(F32)
        q_pow, k_pow = ridx + 1.0, C - 1.0 - ridx
    dist = jnp.where(mask, dist, 0.0)

    def rotate(x):
        x1, x2 = x[:, :half], x[:, half:]
        return jnp.concatenate([x1 * cos - x2 * sin, x1 * sin + x2 * cos], axis=1)

    for hh in range(heads_per_step):
        lg = lg_ref[pl.program_id(1) * heads_per_step + hh]
        sl = slice(hh * hd, (hh + 1) * hd)
        q = rotate(q_ref[:, sl])
        k = rotate(k_ref[:, sl]) * (hd ** -0.5)
        v = v_ref[:, sl].astype(BF16)
        intra = jnp.where(mask, jnp.exp(lg * dist), 0.0)
        s = lax.dot_general(q.astype(BF16), k.astype(BF16), (((1,), (1,)), ((), ())),
                            preferred_element_type=F32) * intra
        state = state_ref[hh]
        o = (jnp.dot(s.astype(BF16), v, preferred_element_type=F32)
             + jnp.dot((q * jnp.exp(lg * q_pow)).astype(BF16), state.astype(BF16),
                       preferred_element_type=F32))
        kd = (k * jnp.exp(lg * k_pow)).astype(BF16)
        state_ref[hh] = state * jnp.exp(lg * jnp.full((1, hd), C, F32)) + lax.dot_general(
            kd, v, (((0,), (0,)), ((), ())), preferred_element_type=F32)
        if final:
            o_ref[:, sl] = _gated_head_norm(op_ref[:, sl] + o, g_ref[:, sl]).astype(o_ref.dtype)
        else:
            o_ref[:, sl] = o


def _ret_pass(z, log_gamma, cos, sin, o_prev, *, reverse):
    B, L, D4 = z.shape
    D = D4 // 4
    hd = D // RET_HEADS
    hps = RET_HEADS_PER_STEP
    W = hps * hd
    C = min(RET_CHUNK, L)
    n = L // C
    nblk = D // W
    final = o_prev is not None
    cidx = (lambda c: n - 1 - c) if reverse else (lambda c: c)

    def zspec(part):
        return pl.BlockSpec((None, C, W), lambda b, h, c, lg: (b, cidx(c), part * nblk + h))

    tspec = pl.BlockSpec((C, hd // 2), lambda b, h, c, lg: (cidx(c), 0))
    ospec = pl.BlockSpec((None, C, W), lambda b, h, c, lg: (b, cidx(c), h))
    operands = [z, z, z, cos, sin]
    in_specs = [zspec(0), zspec(1), zspec(2), tspec, tspec]
    if final:
        operands += [o_prev, z]
        in_specs += [ospec, zspec(3)]
    kern = functools.partial(_ret_kernel, reverse=reverse, final=final, heads_per_step=hps, hd=hd)
    return pl.pallas_call(
        kern,
        out_shape=jax.ShapeDtypeStruct((B, L, D), BF16 if final else F32),
        grid_spec=pltpu.PrefetchScalarGridSpec(
            num_scalar_prefetch=1,
            grid=(B, nblk, n),
            in_specs=in_specs,
            out_specs=ospec,
            scratch_shapes=[pltpu.VMEM((hps, hd, hd), F32)]),
        compiler_params=_cparams("parallel", "parallel", "arbitrary"),
        name="ret_bwd" if reverse else "ret_fwd",
    )(log_gamma, *operands)


def _pair_split_perm(d, heads):
    hd = d // heads
    idx = np.arange(d).reshape(heads, hd)
    return np.concatenate([idx[:, 0::2], idx[:, 1::2]], axis=1).reshape(-1)


def _retention_mixer(h, x, g1, p):
    B, L, D = x.shape
    hd = D // RET_HEADS
    z = _matmul(h.reshape(B * L, D), p["w_in"], layer=p["slot"], name="ret_in").reshape(B, L, 4 * D)
    pos = jnp.arange(L, dtype=F32)
    inv = 1.0 / (ROPE_BASE ** jnp.linspace(0.0, 1.0, hd // 2, dtype=F32))
    ang = pos[:, None] * inv[None, :]
    cos, sin = jnp.cos(ang), jnp.sin(ang)
    hidx = jnp.arange(RET_HEADS, dtype=F32)
    lg_fwd = jnp.log1p(-jnp.exp2(-5.0 - hidx))
    lg_bwd = jnp.log1p(-jnp.exp2(-5.0 - hidx[::-1]))
    o_fwd = _ret_pass(z, lg_fwd, cos, sin, None, reverse=False)
    y = _ret_pass(z, lg_bwd, cos, sin, o_fwd, reverse=True)
    out = _matmul(y.reshape(B * L, D), p["w_out"], layer=p["slot"], resid=x.reshape(B * L, D), gate=g1,
                  rows_per_gate=L, name="ret_out", **MM_OUT)
    return out.reshape(B, L, D)


def _gla_kernel(q_ref, k_ref, v_ref, t_ref, w2_ref, gb_ref, *rest, reverse, final, sub):
    if final:
        op_ref, r_ref, o_ref, state_ref, a_ref = rest
    else:
        o_ref, state_ref, a_ref = rest
    C, dk = q_ref.shape

    @pl.when(pl.program_id(2) == 0)
    def _():
        state_ref[...] = jnp.zeros_like(state_ref)

    zg = jnp.dot(t_ref[...].astype(BF16), w2_ref[...].astype(BF16), preferred_element_type=F32) + gb_ref[...]
    log_a = (jnp.minimum(zg, 0.0) - jnp.log1p(jnp.exp(-jnp.abs(zg)))) * (1.0 / GLA_TAU)

    row = lax.broadcasted_iota(jnp.int32, (C, C), 0)
    col = lax.broadcasted_iota(jnp.int32, (C, C), 1)
    tri = ((col >= row) if reverse else (col <= row)).astype(BF16)
    hi = log_a.astype(BF16)
    lo = (log_a - hi.astype(F32)).astype(BF16)
    b = (jnp.dot(tri, hi, preferred_element_type=F32) + jnp.dot(tri, lo, preferred_element_type=F32))
    b_tot = jnp.sum(log_a, axis=0, keepdims=True)

    q = q_ref[...] * (dk ** -0.5)
    k = k_ref[...]
    v = v_ref[...].astype(BF16)

    for blk in range(C // sub):
        rows = slice(blk * sub, (blk + 1) * sub)
        ref = b[blk * sub + sub // 2:blk * sub + sub // 2 + 1, :]
        qh = q[rows] * jnp.exp(jnp.minimum(b[rows] - ref, GLA_EXP_CLAMP))
        kh = k * jnp.exp(jnp.minimum(ref - b, GLA_EXP_CLAMP))
        s = lax.dot_general(qh.astype(BF16), kh.astype(BF16), (((1,), (1,)), ((), ())),
                            preferred_element_type=F32)
        r_i = lax.broadcasted_iota(jnp.int32, (sub, C), 0) + blk * sub
        c_i = lax.broadcasted_iota(jnp.int32, (sub, C), 1)
        keep = (c_i > r_i) if reverse else (c_i <= r_i)
        a_ref[rows, :] = jnp.where(keep, s, 0.0).astype(BF16)

    state = state_ref[...]
    o = (jnp.dot(a_ref[...], v, preferred_element_type=F32)
         + lax.dot_general((q * jnp.exp(b)).astype(BF16), state.astype(BF16), (((1,), (1,)), ((), ())),
                           preferred_element_type=F32))
    kd = (k * jnp.exp(b_tot - b)).astype(BF16)
    state_ref[...] = state * jnp.exp(b_tot) + lax.dot_general(
        v, kd, (((0,), (0,)), ((), ())), preferred_element_type=F32)
    if final:
        o_ref[...] = _gated_head_norm(op_ref[...] + o, r_ref[...]).astype(o_ref.dtype)
    else:
        o_ref[...] = o


def _gla_pass(z, t, w2, gb, o_prev, *, direction):
    B, L, D3 = z.shape
    D = D3 // 3
    dk = D // 2 // GLA_HEADS
    dv = D // GLA_HEADS
    C = min(GLA_CHUNK, L)
    n = L // C
    reverse = direction == 1
    final = o_prev is not None
    cidx = (lambda c: n - 1 - c) if reverse else (lambda c: c)
    H = GLA_HEADS

    qspec = pl.BlockSpec((None, C, dk), lambda b, h, c: (b, cidx(c), h))
    kspec = pl.BlockSpec((None, C, dk), lambda b, h, c: (b, cidx(c), H + h))
    vspec = pl.BlockSpec((None, C, dv), lambda b, h, c: (b, cidx(c), H + h))
    rspec = pl.BlockSpec((None, C, dv), lambda b, h, c: (b, cidx(c), 2 * H + h))
    ospec = pl.BlockSpec((None, C, dv), lambda b, h, c: (b, cidx(c), h))
    tspec = pl.BlockSpec((None, C, t.shape[-1]), lambda b, h, c: (b, cidx(c), 0))
    w2spec = pl.BlockSpec((None, t.shape[-1], dk), lambda b, h, c: (direction, 0, h))
    gbspec = pl.BlockSpec((None, 1, dk), lambda b, h, c: (direction, 0, h))
    operands = [z, z, z, t, w2, gb]
    in_specs = [qspec, kspec, vspec, tspec, w2spec, gbspec]
    if final:
        operands += [o_prev, z]
        in_specs += [ospec, rspec]
    kern = functools.partial(_gla_kernel, reverse=reverse, final=final, sub=min(GLA_SUBBLOCK, C))
    return pl.pallas_call(
        kern,
        out_shape=jax.ShapeDtypeStruct((B, L, D), BF16 if final else F32),
        grid=(B, H, n),
        in_specs=in_specs,
        out_specs=ospec,
        scratch_shapes=[pltpu.VMEM((dv, dk), F32), pltpu.VMEM((C, C), BF16)],
        compiler_params=_cparams("parallel", "parallel", "arbitrary"),
        name="gla_bwd" if reverse else "gla_fwd",
    )(*operands)


def _gla_mixer(h, x, g1, p):
    B, L, D = x.shape
    h2 = h.reshape(B * L, D)
    z = _matmul(h2, p["w_in"], layer=p["slot"], name="gla_in").reshape(B, L, 3 * D)
    t = _matmul(h2, p["gate_w1"], layer=p["slot"], bn=V7X_LANES, name="gla_gate").reshape(B, L, V7X_LANES)
    o_fwd = _gla_pass(z, t, p["gate_w2"], p["gate_b"], None, direction=0)
    y = _gla_pass(z, t, p["gate_w2"], p["gate_b"], o_fwd, direction=1)
    out = _matmul(y.reshape(B * L, D), p["w_out"], layer=p["slot"], resid=x.reshape(B * L, D), gate=g1,
                  rows_per_gate=L, name="gla_out", **MM_OUT)
    return out.reshape(B, L, D)


def _conv_gate_kernel(*refs, rows):
    groups = [refs[5 * g:5 * g + 5] for g in range(3)]
    x0_out, vv_out, pad_ref = refs[15:]
    i = pl.program_id(1)
    first = i == 0
    last = i == pl.num_programs(1) - 1

    def conv(cur_ref, prev_ref, next_ref, w_ref, b_ref):
        pad_ref[pl.ds(8, rows), :] = cur_ref[...]
        pad_ref[pl.ds(0, 8), :] = jnp.where(first, 0.0, prev_ref[...])
        pad_ref[pl.ds(rows + 8, 8), :] = jnp.where(last, 0.0, next_ref[...])
        w = w_ref[...]
        return (pad_ref[pl.ds(7, rows), :] * w[0:1] + cur_ref[...] * w[1:2]
                + pad_ref[pl.ds(9, rows), :] * w[2:3] + b_ref[...])

    x0_out[...] = conv(*groups[0])
    x1 = conv(*groups[1])
    vv_out[...] = conv(*groups[2]) * x1


def _conv_gate(z, conv_w, conv_b):
    B, L, D3 = z.shape
    D = D3 // 3
    rows, ct = min(512, L), 512
    nct = D // ct
    nhalo = L // 8
    operands, in_specs = [], []
    for g in range(3):
        operands += [z, z, z, conv_w, conv_b.reshape(1, D3)]
        in_specs += [
            pl.BlockSpec((None, rows, ct), lambda b, i, j, g=g: (b, i, g * nct + j)),
            pl.BlockSpec((None, 8, ct), lambda b, i, j, g=g: (b, jnp.maximum(i * (rows // 8) - 1, 0), g * nct + j)),
            pl.BlockSpec((None, 8, ct), lambda b, i, j, g=g: (b, jnp.minimum((i + 1) * (rows // 8), nhalo - 1), g * nct + j)),
            pl.BlockSpec((3, ct), lambda b, i, j, g=g: (0, g * nct + j)),
            pl.BlockSpec((1, ct), lambda b, i, j, g=g: (0, g * nct + j)),
        ]
    ospec = pl.BlockSpec((None, rows, ct), lambda b, i, j: (b, i, j))
    return pl.pallas_call(
        functools.partial(_conv_gate_kernel, rows=rows),
        out_shape=(jax.ShapeDtypeStruct((B, L, D), F32), jax.ShapeDtypeStruct((B, L, D), F32)),
        grid=(B, L // rows, nct),
        in_specs=in_specs,
        out_specs=(ospec, ospec),
        scratch_shapes=[pltpu.VMEM((rows + 16, ct), F32)],
        compiler_params=_cparams("parallel", "parallel", "parallel"),
        name="hy_conv_gate",
    )(*operands)


FFT_GROUP = 16
FFT_K1_GROUP = 8
FFT_LANES = 256


def _stage_a(x_parts, g_ref, z_ref, stage_ref, step, *, n1, ng, mirror_second=False):
    xt = [jnp.swapaxes(x, 0, 1) for x in x_parts]
    for j in range(FFT_GROUP):
        x = jnp.concatenate([xt[0][j], xt[1][FFT_GROUP - 1 - j if mirror_second else j]], axis=0)
        stage_ref[j] = jnp.dot(g_ref[j], x.astype(BF16), preferred_element_type=F32)
    a = jnp.swapaxes(stage_ref[...], 0, 1)
    z_ref[:, step] = a[:n1].astype(z_ref.dtype)
    z_ref[:, ng + step] = a[n1:].astype(z_ref.dtype)


def _fft_conv_kernel(x_ref, x0_ref, g_ref, f_ref, fi_ref, kf_ref, gi_ref, skip_ref, o_ref, z_ref, sa_ref, sc_ref,
                     *, sa, sb, n1, n2, h1):
    step = pl.program_id(2)
    ng = n2 // FFT_GROUP

    @pl.when(step < sa)
    def _():
        _stage_a([x_ref[0], x_ref[1]], g_ref, z_ref, sa_ref, step, n1=n1, ng=ng)

    @pl.when(jnp.logical_and(step >= sa, step < sa + sb))
    def _():
        for j in range(FFT_K1_GROUP):
            k1 = (step - sa) * FFT_K1_GROUP + j
            zk = z_ref[k1].reshape(2 * n2, z_ref.shape[-1])
            y = jnp.dot(f_ref[...], zk, preferred_element_type=F32)
            yr, yi = y[:n2], y[n2:]
            kr, ki = kf_ref[j, :n2, :], kf_ref[j, n2:, :]
            p = jnp.concatenate([yr * kr - yi * ki, yr * ki + yi * kr], axis=0).astype(BF16)
            back = jnp.dot(fi_ref[...], p, preferred_element_type=F32)
            z_ref[k1] = back.astype(z_ref.dtype).reshape(z_ref.shape[1:])

    @pl.when(step >= sa + sb)
    def _():
        g = step - sa - sb
        zr = jnp.swapaxes(z_ref[:, g].astype(F32), 0, 1)
        zi = jnp.swapaxes(z_ref[:, ng + g].astype(F32), 0, 1)
        for j in range(FFT_GROUP):
            bz = jnp.concatenate([zr[j], zi[j]], axis=0).astype(BF16)
            sc_ref[j] = jnp.dot(gi_ref[j], bz, preferred_element_type=F32)
        y = jnp.swapaxes(sc_ref[...], 0, 1).reshape(x_ref.shape)
        o_ref[...] = ((y + skip_ref[...] * x_ref[...]) * x0_ref[...]).astype(o_ref.dtype)


def _dft_tables(L):
    N = 2 * L
    n1 = n2 = int(round(math.sqrt(N)))
    assert n1 * n2 == N and n2 % FFT_GROUP == 0 and n1 % FFT_K1_GROUP == 0
    h1 = n1 // 2
    k1 = jnp.arange(n1, dtype=jnp.int32)
    m = (jnp.arange(n2, dtype=jnp.int32)[:, None, None] * k1[None, :, None]
         + n2 * k1[None, :, None] * jnp.arange(h1, dtype=jnp.int32)[None, None, :]) % N
    ang = (2.0 * math.pi / N) * m.astype(F32)
    c, s = jnp.cos(ang), jnp.sin(ang)
    block = lambda re, im: jnp.concatenate([jnp.concatenate([re, -im], -1), jnp.concatenate([im, re], -1)], -2)
    g = block(c, -s).astype(BF16)
    ct_, st_ = jnp.swapaxes(c, 1, 2) / N, jnp.swapaxes(s, 1, 2) / N
    gi = block(ct_, st_).astype(BF16)
    idx = jnp.arange(n2, dtype=jnp.int32)
    ang2 = (2.0 * math.pi / n2) * ((idx[:, None] * idx[None, :]) % n2).astype(F32)
    c2, s2 = jnp.cos(ang2), jnp.sin(ang2)
    return dict(n1=n1, n2=n2, h1=h1, g=g, gi=gi, f=block(c2, -s2).astype(BF16), fi=block(c2, s2).astype(BF16))


def _filter_spectrum_kernel(hf_ref, hb_ref, g_ref, f_ref, sum_ref, o_ref, z_ref, sa_ref, *, sa, n1, n2):
    step = pl.program_id(1)
    ng = n2 // FFT_GROUP

    @pl.when(step < sa)
    def _():
        _stage_a([hf_ref[...], hb_ref[...]], g_ref, z_ref, sa_ref, step, n1=n1, ng=ng, mirror_second=True)

    @pl.when(step >= sa)
    def _():
        scale = 1.0 / (sum_ref[...] + EPS)
        for j in range(FFT_K1_GROUP):
            k1 = (step - sa) * FFT_K1_GROUP + j
            zk = z_ref[k1].reshape(2 * n2, z_ref.shape[-1])
            o_ref[j] = jnp.dot(f_ref[...], zk, preferred_element_type=F32) * scale


def _filter_tables(tab):
    n1, n2, h1 = tab["n1"], tab["n2"], tab["h1"]
    N = n1 * n2
    k1 = jnp.arange(n1, dtype=jnp.int32)[None, :, None]
    i2 = jnp.arange(n2, dtype=jnp.int32)[:, None, None]
    a = jnp.arange(h1, dtype=jnp.int32)[None, None, :]
    rows = jnp.concatenate([a, n1 - 1 - a], axis=-1)
    m = (i2 * k1 + n2 * k1 * rows) % N
    ang = (2.0 * math.pi / N) * m.astype(F32)
    col = jnp.arange(2 * h1, dtype=jnp.int32)[None, None, :]
    live = jnp.logical_not(jnp.logical_and(i2 == 0, col == 2 * h1 - 1))
    re = jnp.where(live, jnp.cos(ang), 0.0)
    im = jnp.where(live, -jnp.sin(ang), 0.0)
    return jnp.concatenate([re, im], axis=1).astype(BF16)


def _filter_spectrum(hf, hb, abs_sum, tab):
    L, D = hf.shape
    n1, n2, h1 = tab["n1"], tab["n2"], tab["h1"]
    ct = FFT_LANES
    sa = n2 // FFT_GROUP
    sb = n1 // FFT_K1_GROUP
    kern = functools.partial(_filter_spectrum_kernel, sa=sa, n1=n1, n2=n2)
    return pl.pallas_call(
        kern,
        out_shape=jax.ShapeDtypeStruct((n1, 2 * n2, D), F32),
        grid=(D // ct, sa + sb),
        in_specs=[
            pl.BlockSpec((h1, FFT_GROUP, ct), lambda t, s: (0, jnp.minimum(s, sa - 1), t)),
            pl.BlockSpec((h1, FFT_GROUP, ct), lambda t, s: (0, sa - 1 - jnp.minimum(s, sa - 1), t)),
            pl.BlockSpec((FFT_GROUP, 2 * n1, 2 * h1), lambda t, s: (jnp.minimum(s, sa - 1), 0, 0)),
            pl.BlockSpec((2 * n2, 2 * n2), lambda t, s: (0, 0)),
            pl.BlockSpec((1, ct), lambda t, s: (0, t)),
        ],
        out_specs=pl.BlockSpec((FFT_K1_GROUP, 2 * n2, ct), lambda t, s: (jnp.maximum(s - sa, 0), 0, t)),
        scratch_shapes=[pltpu.VMEM((n1, 2 * n2 // FFT_GROUP, FFT_GROUP, ct), BF16),
                        pltpu.VMEM((FFT_GROUP, 2 * n1, ct), F32)],
        compiler_params=_cparams("parallel", "arbitrary"),
        name="hy_filter_spectrum",
    )(hf.reshape(h1, n2, D), hb.reshape(h1, n2, D), tab["gf"], tab["f"], abs_sum)


def _fft_conv(vv, x0, kf, skip, tab):
    B, L, D = vv.shape
    n1, n2, h1 = tab["n1"], tab["n2"], tab["h1"]
    assert B % 2 == 0 and L == h1 * n2
    ct = FFT_LANES
    sa = n2 // FFT_GROUP
    sb = n1 // FFT_K1_GROUP
    sc = sa
    shape5 = (B // 2, 2, h1, n2, D)
    a_idx = lambda s: jnp.where(s < sa, s, jnp.where(s >= sa + sb, s - sa - sb, sa - 1))
    c_idx = lambda s: jnp.clip(s - sa - sb, 0, sc - 1)
    blk = (None, 2, h1, FFT_GROUP, ct)
    in_specs = [
        pl.BlockSpec(blk, lambda p, t, s: (p, 0, 0, a_idx(s), t)),
        pl.BlockSpec(blk, lambda p, t, s: (p, 0, 0, c_idx(s), t)),
        pl.BlockSpec((FFT_GROUP, 2 * n1, 2 * h1), lambda p, t, s: (jnp.minimum(s, sa - 1), 0, 0)),
        pl.BlockSpec((2 * n2, 2 * n2), lambda p, t, s: (0, 0)),
        pl.BlockSpec((2 * n2, 2 * n2), lambda p, t, s: (0, 0)),
        pl.BlockSpec((FFT_K1_GROUP, 2 * n2, ct), lambda p, t, s: (jnp.clip(s - sa, 0, sb - 1), 0, t)),
        pl.BlockSpec((FFT_GROUP, 2 * h1, 2 * n1), lambda p, t, s: (c_idx(s), 0, 0)),
        pl.BlockSpec((1, ct), lambda p, t, s: (0, t)),
    ]
    kern = functools.partial(_fft_conv_kernel, sa=sa, sb=sb, n1=n1, n2=n2, h1=h1)
    y = pl.pallas_call(
        kern,
        out_shape=jax.ShapeDtypeStruct(shape5, BF16),
        grid=(B // 2, D // ct, sa + sb + sc),
        in_specs=in_specs,
        out_specs=pl.BlockSpec(blk, lambda p, t, s: (p, 0, 0, c_idx(s), t)),
        scratch_shapes=[pltpu.VMEM((n1, 2 * n2 // FFT_GROUP, FFT_GROUP, ct), BF16),
                        pltpu.VMEM((FFT_GROUP, 2 * n1, ct), F32),
                        pltpu.VMEM((FFT_GROUP, 2 * h1, ct), F32)],
        compiler_params=_cparams("parallel", "parallel", "arbitrary"),
        name="hy_fft_conv",
    )(vv.reshape(shape5), x0.reshape(shape5), tab["g"], tab["f"], tab["fi"], kf, tab["gi"],
      skip.reshape(1, D))
    return y.reshape(B, L, D)


def _filter_taps_kernel(feat_ref, t_ref, wf_ref, wb_ref, dl_ref, hf_ref, hb_ref, sum_ref, *, rows, total_rows):
    i = pl.program_id(1)
    window = jnp.exp(-t_ref[...] * dl_ref[...]) + HY_SHIFT
    feat = feat_ref[...]
    hf = jnp.dot(feat, wf_ref[...], preferred_element_type=F32, precision=lax.Precision.HIGHEST) * window
    hb = jnp.dot(feat, wb_ref[...], preferred_element_type=F32, precision=lax.Precision.HIGHEST) * window
    hf_ref[...] = hf
    hb_ref[...] = hb
    row = lax.broadcasted_iota(jnp.int32, hb.shape, 0) + i * rows
    part = (jnp.sum(jnp.abs(hf), axis=0, keepdims=True)
            + jnp.sum(jnp.where(row < total_rows - 1, jnp.abs(hb), 0.0), axis=0, keepdims=True))

    @pl.when(i == 0)
    def _():
        sum_ref[...] = part

    @pl.when(i > 0)
    def _():
        sum_ref[...] += part


def _filter_taps(L, w1, b1, w2, b2, w3, b3, w_out, freq):
    hp = lax.Precision.HIGHEST
    D = w_out.shape[-1] // 2
    t = jnp.linspace(0.0, 1.0, L, dtype=F32)[:, None]
    w = (2.0 * math.pi / L) * jnp.arange(L, dtype=F32)[:, None]
    bands = jnp.linspace(1e-4, HY_BANDS - 1, HY_BANDS, dtype=F32)[None, :]
    z = jnp.concatenate([t, jnp.cos(bands * w), -jnp.sin(bands * w)], axis=-1)
    h = jnp.sin(freq * (jnp.dot(z, w1, precision=hp) + b1))
    h = jnp.sin(freq * (jnp.dot(h, w2, precision=hp) + b2))
    h = jnp.sin(freq * (jnp.dot(h, w3, precision=hp) + b3))
    max_decay = math.log(HY_TARGET) / HY_FAST_DECAY
    min_decay = math.log(HY_TARGET) / HY_SLOW_DECAY
    deltas = jnp.abs(jnp.linspace(min_decay, max_decay, D, dtype=F32))[None, :]
    rows, ct = min(512, L), 512
    width = h.shape[-1]
    nct = D // ct
    tap_spec = pl.BlockSpec((rows, ct), lambda j, i: (i, j))
    return pl.pallas_call(
        functools.partial(_filter_taps_kernel, rows=rows, total_rows=L),
        out_shape=(jax.ShapeDtypeStruct((L, D), F32), jax.ShapeDtypeStruct((L, D), F32),
                   jax.ShapeDtypeStruct((1, D), F32)),
        grid=(nct, L // rows),
        in_specs=[pl.BlockSpec((rows, width), lambda j, i: (i, 0)),
                  pl.BlockSpec((rows, 1), lambda j, i: (i, 0)),
                  pl.BlockSpec((width, ct), lambda j, i: (0, j)),
                  pl.BlockSpec((width, ct), lambda j, i: (0, nct + j)),
                  pl.BlockSpec((1, ct), lambda j, i: (0, j))],
        out_specs=(tap_spec, tap_spec, pl.BlockSpec((1, ct), lambda j, i: (0, j))),
        compiler_params=_cparams("parallel", "arbitrary"),
        name="hy_filter_taps",
    )(h, t, w_out, w_out, deltas)


def _hyena_mixer(h, x, g1, p, tab):
    B, L, D = x.shape
    z = _matmul(h.reshape(B * L, D), p["w_in"], layer=p["slot"], bias=p["b_in"], name="hy_in").reshape(B, L, 3 * D)
    x0, vv = _conv_gate(z, p["conv_w"], p["conv_b"])
    hf, hb, abs_sum = _filter_taps(L, p["f_w1"], p["f_b1"], p["f_w2"], p["f_b2"], p["f_w3"], p["f_b3"],
                                   p["f_wout"], p["freq"])
    y = _fft_conv(vv, x0, _filter_spectrum(hf, hb, abs_sum, tab), p["skip"], tab)
    out = _matmul(y.reshape(B * L, D), p["w_out"], layer=p["slot"], bias=p["b_out"],
                  resid=x.reshape(B * L, D), gate=g1, rows_per_gate=L, name="hy_out", **MM_OUT)
    return out.reshape(B, L, D)


STACKED_WEIGHTS = ("w_in", "w_out", "gate_w1")


def _layer_params(params, slot):
    p = {n: (v if n in STACKED_WEIGHTS else v[slot]) for n, v in params.items()}
    p["slot"] = slot
    return p


def _trunk(x, mod, hy, ret, gla, norm_g, mlp_w1, mlp_w2, final_g):
    B, L, D = x.shape
    tab = _dft_tables(L)
    tab["gf"] = _filter_tables(tab)
    for i in range(DEPTH):
        sh1, sc1, g1, sh2, sc2, g2 = jnp.split(mod[i], 6, axis=-1)
        h = _norm(x, norm_g[i, 0], sc1, sh1, out_dtype=BF16, name="norm_mix")
        kind, slot = i % N_MIXERS, i // N_MIXERS
        if kind == 0:
            x = _hyena_mixer(h, x, g1, _layer_params(hy, slot), tab)
        elif kind == 1:
            x = _retention_mixer(h, x, g1, _layer_params(ret, slot))
        else:
            x = _gla_mixer(h, x, g1, _layer_params(gla, slot))
        h = _norm(x, norm_g[i, 1], sc2, sh2, out_dtype=BF16, name="norm_mlp")
        a = _matmul(h.reshape(B * L, D), mlp_w1, layer=i, act="relu2", out_dtype=BF16, name="mlp_up")
        x = _matmul_deep(a, mlp_w2, i, x.reshape(B * L, D), g2, L, name="mlp_down").reshape(B, L, D)
    return _norm(x, final_g, out_dtype=F32, name="norm_final")


def kernel(x_prompt, x_sample, c_prompt, c_sample, hy_w_in, hy_b_in, hy_conv_w, hy_conv_b, hy_f_w1, hy_f_b1, hy_f_w2, hy_f_b2, hy_f_w3, hy_f_b3, hy_f_wout, hy_freq, hy_skip, hy_w_out, hy_b_out, ret_w_in, ret_w_out, gla_w_in, gla_gate_w1, gla_gate_w2, gla_gate_b, gla_w_out, norm_g, ada_w, ada_b, mlp_w1, mlp_w2, final_g):
    D = x_prompt.shape[-1]
    bf = lambda w: w.astype(BF16)
    hy = dict(w_in=bf(hy_w_in), b_in=hy_b_in, conv_w=hy_conv_w, conv_b=hy_conv_b,
              f_w1=hy_f_w1, f_b1=hy_f_b1, f_w2=hy_f_w2, f_b2=hy_f_b2, f_w3=hy_f_w3, f_b3=hy_f_b3,
              f_wout=hy_f_wout, freq=hy_freq, skip=hy_skip, w_out=bf(hy_w_out), b_out=hy_b_out)

    perm = _pair_split_perm(D, RET_HEADS)
    cols = np.concatenate([perm, D + perm, np.arange(2 * D, 4 * D)])
    ret = dict(w_in=bf(ret_w_in[:, :, cols]), w_out=bf(ret_w_out))

    n_gla, _, _, rank = gla_gate_w1.shape
    w1 = jnp.concatenate([gla_gate_w1[:, 0], gla_gate_w1[:, 1],
                          jnp.zeros((n_gla, D, V7X_LANES - 2 * rank), F32)], axis=-1)
    w2 = jnp.zeros((n_gla, 2, V7X_LANES, gla_gate_w2.shape[-1]), F32)
    w2 = w2.at[:, 0, :rank].set(gla_gate_w2[:, 0]).at[:, 1, rank:2 * rank].set(gla_gate_w2[:, 1])
    gla = dict(w_in=bf(gla_w_in), gate_w1=bf(w1), gate_w2=w2, gate_b=gla_gate_b[:, :, None, :],
               w_out=bf(gla_w_out))
    mw1, mw2 = bf(mlp_w1), bf(mlp_w2)

    nb_p, nb_s = c_prompt.shape[0], c_sample.shape[0]
    cs = jax.nn.silu(jnp.concatenate([c_prompt, c_sample], axis=0))
    rows = -(-cs.shape[0] // 16) * 16
    cs = jnp.pad(cs, ((0, rows - cs.shape[0]), (0, 0))).astype(BF16)
    mod = jnp.stack([_matmul(cs, ada_w, layer=i, bias=ada_b[i], bn=512, name="ada") for i in range(DEPTH)])
    mod_p, mod_s = mod[:, :nb_p], mod[:, nb_p:nb_p + nb_s]

    y_prompt = _trunk(x_prompt, mod_p, hy, ret, gla, norm_g, mw1, mw2, final_g)
    y_sample = _trunk(x_sample, mod_s, hy, ret, gla, norm_g, mw1, mw2, final_g)
    return (y_prompt, y_sample)
```

```python
import functools
import math

import jax
import jax.numpy as jnp
import numpy as np
from jax import lax
from jax.experimental import pallas as pl
from jax.experimental.pallas import tpu as pltpu

D_MODEL = 4096
DEPTH = 4
N_MIXERS = 3
EPS = 1e-6
HY_BANDS = 16
HY_FAST_DECAY = 0.3
HY_SLOW_DECAY = 1.5
HY_TARGET = 1e-2
HY_SHIFT = 0.05
RET_HEADS = 16
ROPE_BASE = 10000.0
GLA_HEADS = 4
GLA_GATE_RANK = 16
GLA_TAU = 16.0

F32 = jnp.float32
BF16 = jnp.bfloat16

V7X_VMEM_LIMIT_BYTES = 56 * 1024 * 1024
V7X_LANES = 128
MM_BLOCK_M = 1024
MM_BLOCK_N = 1024
MM_BLOCK_K = 4096
NORM_BLOCK_ROWS = 512
RET_CHUNK = 256
RET_HEADS_PER_STEP = 4
GLA_CHUNK = 128
GLA_SUBBLOCK = 32
GLA_EXP_CLAMP = 80.0


def _cparams(*sem):
    return pltpu.CompilerParams(dimension_semantics=sem, vmem_limit_bytes=V7X_VMEM_LIMIT_BYTES)


def _mm_kernel(*refs, nk, has_bias, has_resid, act):
    it = iter(refs)
    x_ref, w_ref = next(it), next(it)
    b_ref = next(it) if has_bias else None
    r_ref = next(it) if has_resid else None
    g_ref = next(it) if has_resid else None
    o_ref = next(it)
    acc_ref = next(it) if nk > 1 else None

    part = jnp.dot(x_ref[...].astype(BF16), w_ref[...].astype(BF16), preferred_element_type=F32)

    def finish(acc):
        if has_bias:
            acc = acc + b_ref[...]
        if act == "relu2":
            acc = jnp.maximum(acc, 0.0)
            acc = acc * acc
        if has_resid:
            acc = r_ref[...] + g_ref[...] * acc
        o_ref[...] = acc.astype(o_ref.dtype)

    if nk == 1:
        finish(part)
    else:
        k = pl.program_id(2)

        @pl.when(k == 0)
        def _():
            acc_ref[...] = part

        @pl.when(jnp.logical_and(k > 0, k < nk - 1))
        def _():
            acc_ref[...] += part

        @pl.when(k == nk - 1)
        def _():
            finish(acc_ref[...] + part)


def _matmul(x, w, *, layer=None, bias=None, act=None, resid=None, gate=None, rows_per_gate=None,
            out_dtype=F32, bm=MM_BLOCK_M, bn=MM_BLOCK_N, bk=MM_BLOCK_K, name="matmul"):
    M, K = x.shape
    K2, N = w.shape[-2:]
    assert K == K2
    bm, bn, bk = min(bm, M, rows_per_gate or M), min(bn, N), min(bk, K)
    assert M % bm == 0 and N % bn == 0 and K % bk == 0
    nk = K // bk
    has_bias = bias is not None
    has_resid = resid is not None

    operands = [x, w]
    in_specs = [pl.BlockSpec((bm, bk), lambda i, j, k: (i, k)),
                pl.BlockSpec((bk, bn), lambda i, j, k: (k, j)) if layer is None else
                pl.BlockSpec((None, bk, bn), lambda i, j, k: (layer, k, j))]
    if has_bias:
        operands.append(bias.reshape(1, N).astype(F32))
        in_specs.append(pl.BlockSpec((1, bn), lambda i, j, k: (0, j)))
    if has_resid:
        assert rows_per_gate % bm == 0
        tiles_per_gate = rows_per_gate // bm
        operands.append(resid)
        in_specs.append(pl.BlockSpec((bm, bn), lambda i, j, k: (i, j)))
        operands.append(gate.reshape(gate.shape[0], 1, N).astype(F32))
        in_specs.append(pl.BlockSpec((None, 1, bn), lambda i, j, k: (i // tiles_per_gate, 0, j)))

    scratch = [pltpu.VMEM((bm, bn), F32)] if nk > 1 else []
    kern = functools.partial(_mm_kernel, nk=nk, has_bias=has_bias, has_resid=has_resid, act=act)
    return pl.pallas_call(
        kern,
        out_shape=jax.ShapeDtypeStruct((M, N), out_dtype),
        grid=(M // bm, N // bn, nk),
        in_specs=in_specs,
        out_specs=pl.BlockSpec((bm, bn), lambda i, j, k: (i, j)),
        scratch_shapes=scratch,
        compiler_params=_cparams("parallel", "parallel", "arbitrary"),
        name=name,
    )(*operands)


def _mm_deep_kernel(x_ref, w_ref, r_ref, g_ref, o_ref, acc_ref, *, nk, bn):
    k = pl.program_id(1)
    cols = pl.ds(pl.multiple_of(pl.program_id(2) * bn, bn), bn)
    part = jnp.dot(x_ref[...], w_ref[...], preferred_element_type=F32)

    @pl.when(k == 0)
    def _():
        acc_ref[:, cols] = part

    @pl.when(jnp.logical_and(k > 0, k < nk - 1))
    def _():
        acc_ref[:, cols] += part

    @pl.when(k == nk - 1)
    def _():
        o_ref[...] = r_ref[...] + g_ref[...] * (acc_ref[:, cols] + part)


def _matmul_deep(x, w, layer, resid, gate, rows_per_gate, *, bm=MM_BLOCK_M, bn=MM_BLOCK_N, bk=2048, name):
    M, K = x.shape
    N = w.shape[-1]
    bm = min(bm, M, rows_per_gate)
    assert M % bm == 0 and N % bn == 0 and K % bk == 0 and rows_per_gate % bm == 0
    nk = K // bk
    assert nk >= 2
    tiles_per_gate = rows_per_gate // bm
    jcol = lambda k, j: jnp.where(k == nk - 1, j, 0)
    return pl.pallas_call(
        functools.partial(_mm_deep_kernel, nk=nk, bn=bn),
        out_shape=jax.ShapeDtypeStruct((M, N), F32),
        grid=(M // bm, nk, N // bn),
        in_specs=[pl.BlockSpec((bm, bk), lambda i, k, j: (i, k)),
                  pl.BlockSpec((None, bk, bn), lambda i, k, j: (layer, k, j)),
                  pl.BlockSpec((bm, bn), lambda i, k, j: (i, jcol(k, j))),
                  pl.BlockSpec((None, 1, bn), lambda i, k, j: (i // tiles_per_gate, 0, jcol(k, j)))],
        out_specs=pl.BlockSpec((bm, bn), lambda i, k, j: (i, jcol(k, j))),
        scratch_shapes=[pltpu.VMEM((bm, N), F32)],
        compiler_params=_cparams("parallel", "arbitrary", "arbitrary"),
        name=name,
    )(x, w, resid, gate.reshape(gate.shape[0], 1, N).astype(F32))


def _norm_kernel(x_ref, g_ref, *rest, modulate):
    if modulate:
        sc_ref, sh_ref, o_ref = rest
    else:
        (o_ref,) = rest
    x = x_ref[...]
    y = x * lax.rsqrt(jnp.mean(x * x, axis=-1, keepdims=True) + EPS) * g_ref[...]
    if modulate:
        y = y * (1.0 + sc_ref[...]) + sh_ref[...]
    o_ref[...] = y.astype(o_ref.dtype)


def _norm(x, g, scale=None, shift=None, *, out_dtype, name):
    B, L, D = x.shape
    rows = min(NORM_BLOCK_ROWS, L)
    assert L % rows == 0
    modulate = scale is not None
    operands = [x, g.reshape(1, D)]
    in_specs = [pl.BlockSpec((None, rows, D), lambda b, i: (b, i, 0)),
                pl.BlockSpec((1, D), lambda b, i: (0, 0))]
    if modulate:
        operands += [scale.reshape(B, 1, D), shift.reshape(B, 1, D)]
        in_specs += [pl.BlockSpec((None, 1, D), lambda b, i: (b, 0, 0))] * 2
    return pl.pallas_call(
        functools.partial(_norm_kernel, modulate=modulate),
        out_shape=jax.ShapeDtypeStruct((B, L, D), out_dtype),
        grid=(B, L // rows),
        in_specs=in_specs,
        out_specs=pl.BlockSpec((None, rows, D), lambda b, i: (b, i, 0)),
        compiler_params=_cparams("parallel", "parallel"),
        name=name,
    )(*operands)


def _silu(x):
    return x / (1.0 + jnp.exp(-x))


def _gated_head_norm(o, gate):
    o = o * lax.rsqrt(jnp.mean(o * o, axis=-1, keepdims=True) + EPS)
    return _silu(gate) * o


def _ret_kernel(lg_ref, q_ref, k_ref, v_ref, cos_ref, sin_ref, *rest, reverse, final, heads_per_step, hd):
    if final:
        op_ref, g_ref, o_ref, state_ref = rest
    else:
        o_ref, state_ref = rest
    C = q_ref.shape[0]
    half = hd // 2

    @pl.when(pl.program_id(2) == 0)
    def _():
        state_ref[...] = jnp.zeros_like(state_ref)

    cos, sin = cos_ref[...], sin_ref[...]
    row = lax.broadcasted_iota(jnp.int32, (C, C), 0)
    col = lax.broadcasted_iota(jnp.int32, (C, C), 1)
    ridx = lax.broadcasted_iota(jnp.int32, (C, 1), 0).astype(F32)
    if reverse:
        mask = col > row
        dist = (col - row).astype(F32)
        q_pow, k_pow = C - ridx, ridx
    else:
        mask = col <= row
        dist = (row - col).astype(F32)
        q_pow, k_pow = ridx + 1.0, C - 1.0 - ridx
    dist = jnp.where(mask, dist, 0.0)

    def rotate(x):
        x1, x2 = x[:, :half], x[:, half:]
        return jnp.concatenate([x1 * cos - x2 * sin, x1 * sin + x2 * cos], axis=1)

    for hh in range(heads_per_step):
        lg = lg_ref[pl.program_id(1) * heads_per_step + hh]
        sl = slice(hh * hd, (hh + 1) * hd)
        q = rotate(q_ref[:, sl].astype(F32))
        k = rotate(k_ref[:, sl].astype(F32)) * (hd ** -0.5)
        v = v_ref[:, sl].astype(BF16)
        intra = jnp.where(mask, jnp.exp(lg * dist), 0.0)
        s = lax.dot_general(q.astype(BF16), k.astype(BF16), (((1,), (1,)), ((), ())),
                            preferred_element_type=F32) * intra
        state = state_ref[hh]
        o = (jnp.dot(s.astype(BF16), v, preferred_element_type=F32)
             + jnp.dot((q * jnp.exp(lg * q_pow)).astype(BF16), state.astype(BF16),
                       preferred_element_type=F32))
        kd = (k * jnp.exp(lg * k_pow)).astype(BF16)
        state_ref[hh] = state * jnp.exp(lg * jnp.full((1, hd), C, F32)) + lax.dot_general(
            kd, v, (((0,), (0,)), ((), ())), preferred_element_type=F32)
        if final:
            o_ref[:, sl] = _gated_head_norm(op_ref[:, sl].astype(F32) + o,
                                            g_ref[:, sl].astype(F32)).astype(o_ref.dtype)
        else:
            o_ref[:, sl] = o.astype(o_ref.dtype)


def _ret_pass(z, log_gamma, cos, sin, o_prev, *, reverse):
    B, L, D4 = z.shape
    D = D4 // 4
    hd = D // RET_HEADS
    hps = RET_HEADS_PER_STEP
    W = hps * hd
    C = min(RET_CHUNK, L)
    n = L // C
    nblk = D // W
    final = o_prev is not None
    cidx = (lambda c: n - 1 - c) if reverse else (lambda c: c)

    def zspec(part):
        return pl.BlockSpec((None, C, W), lambda b, h, c, lg: (b, cidx(c), part * nblk + h))

    tspec = pl.BlockSpec((C, hd // 2), lambda b, h, c, lg: (cidx(c), 0))
    ospec = pl.BlockSpec((None, C, W), lambda b, h, c, lg: (b, cidx(c), h))
    operands = [z, z, z, cos, sin]
    in_specs = [zspec(0), zspec(1), zspec(2), tspec, tspec]
    if final:
        operands += [o_prev, z]
        in_specs += [ospec, zspec(3)]
    kern = functools.partial(_ret_kernel, reverse=reverse, final=final, heads_per_step=hps, hd=hd)
    return pl.pallas_call(
        kern,
        out_shape=jax.ShapeDtypeStruct((B, L, D), BF16),
        grid_spec=pltpu.PrefetchScalarGridSpec(
            num_scalar_prefetch=1,
            grid=(B, nblk, n),
            in_specs=in_specs,
            out_specs=ospec,
            scratch_shapes=[pltpu.VMEM((hps, hd, hd), F32)]),
        compiler_params=_cparams("parallel", "parallel", "arbitrary"),
        name="ret_bwd" if reverse else "ret_fwd",
    )(log_gamma, *operands)


def _pair_split_perm(d, heads):
    hd = d // heads
    idx = np.arange(d).reshape(heads, hd)
    return np.concatenate([idx[:, 0::2], idx[:, 1::2]], axis=1).reshape(-1)


def _retention_mixer(h, x, g1, p):
    B, L, D = x.shape
    hd = D // RET_HEADS
    z = _matmul(h.reshape(B * L, D), p["w_in"], layer=p["slot"], out_dtype=BF16,
                name="ret_in").reshape(B, L, 4 * D)
    pos = jnp.arange(L, dtype=F32)
    inv = 1.0 / (ROPE_BASE ** jnp.linspace(0.0, 1.0, hd // 2, dtype=F32))
    ang = pos[:, None] * inv[None, :]
    cos, sin = jnp.cos(ang), jnp.sin(ang)
    hidx = jnp.arange(RET_HEADS, dtype=F32)
    lg_fwd = jnp.log1p(-jnp.exp2(-5.0 - hidx))
    lg_bwd = jnp.log1p(-jnp.exp2(-5.0 - hidx[::-1]))
    o_fwd = _ret_pass(z, lg_fwd, cos, sin, None, reverse=False)
    y = _ret_pass(z, lg_bwd, cos, sin, o_fwd, reverse=True)
    out = _matmul(y.reshape(B * L, D), p["w_out"], layer=p["slot"], resid=x.reshape(B * L, D), gate=g1,
                  rows_per_gate=L, name="ret_out")
    return out.reshape(B, L, D)


def _gla_kernel(q_ref, k_ref, v_ref, t_ref, w2_ref, gb_ref, *rest, reverse, final, sub):
    if final:
        op_ref, r_ref, o_ref, state_ref, a_ref = rest
    else:
        o_ref, state_ref, a_ref = rest
    C, dk = q_ref.shape

    @pl.when(pl.program_id(2) == 0)
    def _():
        state_ref[...] = jnp.zeros_like(state_ref)

    zg = jnp.dot(t_ref[...].astype(BF16), w2_ref[...].astype(BF16), preferred_element_type=F32) + gb_ref[...]
    log_a = (jnp.minimum(zg, 0.0) - jnp.log1p(jnp.exp(-jnp.abs(zg)))) * (1.0 / GLA_TAU)

    row = lax.broadcasted_iota(jnp.int32, (C, C), 0)
    col = lax.broadcasted_iota(jnp.int32, (C, C), 1)
    tri = ((col >= row) if reverse else (col <= row)).astype(BF16)
    hi = log_a.astype(BF16)
    lo = (log_a - hi.astype(F32)).astype(BF16)
    b = (jnp.dot(tri, hi, preferred_element_type=F32) + jnp.dot(tri, lo, preferred_element_type=F32))
    b_tot = jnp.sum(log_a, axis=0, keepdims=True)

    q = q_ref[...].astype(F32) * (dk ** -0.5)
    k = k_ref[...].astype(F32)
    v = v_ref[...].astype(BF16)

    for blk in range(C // sub):
        rows = slice(blk * sub, (blk + 1) * sub)
        ref = b[blk * sub + sub // 2:blk * sub + sub // 2 + 1, :]
        qh = q[rows] * jnp.exp(jnp.minimum(b[rows] - ref, GLA_EXP_CLAMP))
        kh = k * jnp.exp(jnp.minimum(ref - b, GLA_EXP_CLAMP))
        s = lax.dot_general(qh.astype(BF16), kh.astype(BF16), (((1,), (1,)), ((), ())),
                            preferred_element_type=F32)
        r_i = lax.broadcasted_iota(jnp.int32, (sub, C), 0) + blk * sub
        c_i = lax.broadcasted_iota(jnp.int32, (sub, C), 1)
        keep = (c_i > r_i) if reverse else (c_i <= r_i)
        a_ref[rows, :] = jnp.where(keep, s, 0.0).astype(BF16)

    state = state_ref[...]
    o = (jnp.dot(a_ref[...], v, preferred_element_type=F32)
         + lax.dot_general((q * jnp.exp(b)).astype(BF16), state.astype(BF16), (((1,), (1,)), ((), ())),
                           preferred_element_type=F32))
    kd = (k * jnp.exp(b_tot - b)).astype(BF16)
    state_ref[...] = state * jnp.exp(b_tot) + lax.dot_general(
        v, kd, (((0,), (0,)), ((), ())), preferred_element_type=F32)
    if final:
        o_ref[...] = _gated_head_norm(op_ref[...].astype(F32) + o, r_ref[...].astype(F32)).astype(o_ref.dtype)
    else:
        o_ref[...] = o.astype(o_ref.dtype)


def _gla_pass(z, t, w2, gb, o_prev, *, direction):
    B, L, D3 = z.shape
    D = D3 // 3
    dk = D // 2 // GLA_HEADS
    dv = D // GLA_HEADS
    C = min(GLA_CHUNK, L)
    n = L // C
    reverse = direction == 1
    final = o_prev is not None
    cidx = (lambda c: n - 1 - c) if reverse else (lambda c: c)
    H = GLA_HEADS

    qspec = pl.BlockSpec((None, C, dk), lambda b, h, c: (b, cidx(c), h))
    kspec = pl.BlockSpec((None, C, dk), lambda b, h, c: (b, cidx(c), H + h))
    vspec = pl.BlockSpec((None, C, dv), lambda b, h, c: (b, cidx(c), H + h))
    rspec = pl.BlockSpec((None, C, dv), lambda b, h, c: (b, cidx(c), 2 * H + h))
    ospec = pl.BlockSpec((None, C, dv), lambda b, h, c: (b, cidx(c), h))
    tspec = pl.BlockSpec((None, C, t.shape[-1]), lambda b, h, c: (b, cidx(c), 0))
    w2spec = pl.BlockSpec((None, t.shape[-1], dk), lambda b, h, c: (direction, 0, h))
    gbspec = pl.BlockSpec((None, 1, dk), lambda b, h, c: (direction, 0, h))
    operands = [z, z, z, t, w2, gb]
    in_specs = [qspec, kspec, vspec, tspec, w2spec, gbspec]
    if final:
        operands += [o_prev, z]
        in_specs += [ospec, rspec]
    kern = functools.partial(_gla_kernel, reverse=reverse, final=final, sub=min(GLA_SUBBLOCK, C))
    return pl.pallas_call(
        kern,
        out_shape=jax.ShapeDtypeStruct((B, L, D), BF16),
        grid=(B, H, n),
        in_specs=in_specs,
        out_specs=ospec,
        scratch_shapes=[pltpu.VMEM((dv, dk), F32), pltpu.VMEM((C, C), BF16)],
        compiler_params=_cparams("parallel", "parallel", "arbitrary"),
        name="gla_bwd" if reverse else "gla_fwd",
    )(*operands)


def _gla_mixer(h, x, g1, p):
    B, L, D = x.shape
    h2 = h.reshape(B * L, D)
    z = _matmul(h2, p["w_in"], layer=p["slot"], out_dtype=BF16, name="gla_in").reshape(B, L, 3 * D)
    t = _matmul(h2, p["gate_w1"], layer=p["slot"], bn=V7X_LANES, name="gla_gate").reshape(B, L, V7X_LANES)
    o_fwd = _gla_pass(z, t, p["gate_w2"], p["gate_b"], None, direction=0)
    y = _gla_pass(z, t, p["gate_w2"], p["gate_b"], o_fwd, direction=1)
    out = _matmul(y.reshape(B * L, D), p["w_out"], layer=p["slot"], resid=x.reshape(B * L, D), gate=g1,
                  rows_per_gate=L, name="gla_out")
    return out.reshape(B, L, D)


CONV_HALO = 16


def _conv_gate_kernel(*refs, rows):
    groups = [refs[5 * g:5 * g + 5] for g in range(3)]
    x0_out, vv_out, pad_ref = refs[15:]
    i = pl.program_id(1)
    first = i == 0
    last = i == pl.num_programs(1) - 1
    h = CONV_HALO

    def conv(cur_ref, prev_ref, next_ref, w_ref, b_ref):
        cur = cur_ref[...].astype(F32)
        pad_ref[pl.ds(h, rows), :] = cur
        pad_ref[pl.ds(0, h), :] = jnp.where(first, 0.0, prev_ref[...].astype(F32))
        pad_ref[pl.ds(rows + h, h), :] = jnp.where(last, 0.0, next_ref[...].astype(F32))
        w = w_ref[...]
        return (pad_ref[pl.ds(h - 1, rows), :] * w[0:1] + cur * w[1:2]
                + pad_ref[pl.ds(h + 1, rows), :] * w[2:3] + b_ref[...])

    x0_out[...] = conv(*groups[0]).astype(x0_out.dtype)
    x1 = conv(*groups[1])
    vv_out[...] = (conv(*groups[2]) * x1).astype(vv_out.dtype)


def _conv_gate(z, conv_w, conv_b):
    B, L, D3 = z.shape
    D = D3 // 3
    rows, ct = min(512, L), 512
    nct = D // ct
    h = CONV_HALO
    nhalo = L // h
    operands, in_specs = [], []
    for g in range(3):
        operands += [z, z, z, conv_w, conv_b.reshape(1, D3)]
        in_specs += [
            pl.BlockSpec((None, rows, ct), lambda b, i, j, g=g: (b, i, g * nct + j)),
            pl.BlockSpec((None, h, ct), lambda b, i, j, g=g: (b, jnp.maximum(i * (rows // h) - 1, 0), g * nct + j)),
            pl.BlockSpec((None, h, ct), lambda b, i, j, g=g: (b, jnp.minimum((i + 1) * (rows // h), nhalo - 1), g * nct + j)),
            pl.BlockSpec((3, ct), lambda b, i, j, g=g: (0, g * nct + j)),
            pl.BlockSpec((1, ct), lambda b, i, j, g=g: (0, g * nct + j)),
        ]
    ospec = pl.BlockSpec((None, rows, ct), lambda b, i, j: (b, i, j))
    return pl.pallas_call(
        functools.partial(_conv_gate_kernel, rows=rows),
        out_shape=(jax.ShapeDtypeStruct((B, L, D), BF16), jax.ShapeDtypeStruct((B, L, D), BF16)),
        grid=(B, L // rows, nct),
        in_specs=in_specs,
        out_specs=(ospec, ospec),
        scratch_shapes=[pltpu.VMEM((rows + 2 * h, ct), F32)],
        compiler_params=_cparams("parallel", "parallel", "parallel"),
        name="hy_conv_gate",
    )(*operands)


FFT_GROUP = 16
FFT_K1_GROUP = 8
FFT_LANES = 256


def _stage_a(x_parts, g_ref, z_ref, stage_ref, step, *, n1, ng, mirror_second=False):
    xt = [jnp.swapaxes(x, 0, 1) for x in x_parts]
    for j in range(FFT_GROUP):
        x = jnp.concatenate([xt[0][j], xt[1][FFT_GROUP - 1 - j if mirror_second else j]], axis=0)
        stage_ref[j] = jnp.dot(g_ref[j], x.astype(BF16), preferred_element_type=F32)
    a = jnp.swapaxes(stage_ref[...], 0, 1)
    z_ref[:, step] = a[:n1].astype(z_ref.dtype)
    z_ref[:, ng + step] = a[n1:].astype(z_ref.dtype)


def _fft_conv_kernel(x_ref, x0_ref, g_ref, f_ref, fi_ref, kf_ref, gi_ref, skip_ref, o_ref, z_ref, sa_ref, sc_ref,
                     *, sa, sb, n1, n2, h1):
    step = pl.program_id(2)
    ng = n2 // FFT_GROUP

    @pl.when(step < sa)
    def _():
        _stage_a([x_ref[0].astype(F32), x_ref[1].astype(F32)], g_ref, z_ref, sa_ref, step, n1=n1, ng=ng)

    @pl.when(jnp.logical_and(step >= sa, step < sa + sb))
    def _():
        for j in range(FFT_K1_GROUP):
            k1 = (step - sa) * FFT_K1_GROUP + j
            zk = z_ref[k1].reshape(2 * n2, z_ref.shape[-1])
            y = jnp.dot(f_ref[...], zk, preferred_element_type=F32)
            yr, yi = y[:n2], y[n2:]
            kr, ki = kf_ref[j, :n2, :], kf_ref[j, n2:, :]
            p = jnp.concatenate([yr * kr - yi * ki, yr * ki + yi * kr], axis=0).astype(BF16)
            back = jnp.dot(fi_ref[...], p, preferred_element_type=F32)
            z_ref[k1] = back.astype(z_ref.dtype).reshape(z_ref.shape[1:])

    @pl.when(step >= sa + sb)
    def _():
        g = step - sa - sb
        zr = jnp.swapaxes(z_ref[:, g].astype(F32), 0, 1)
        zi = jnp.swapaxes(z_ref[:, ng + g].astype(F32), 0, 1)
        for j in range(FFT_GROUP):
            bz = jnp.concatenate([zr[j], zi[j]], axis=0).astype(BF16)
            sc_ref[j] = jnp.dot(gi_ref[j], bz, preferred_element_type=F32)
        y = jnp.swapaxes(sc_ref[...], 0, 1).reshape(x_ref.shape)
        o_ref[...] = ((y + skip_ref[...] * x_ref[...].astype(F32)) * x0_ref[...].astype(F32)).astype(o_ref.dtype)


def _dft_tables(L):
    N = 2 * L
    n1 = n2 = int(round(math.sqrt(N)))
    assert n1 * n2 == N and n2 % FFT_GROUP == 0 and n1 % FFT_K1_GROUP == 0
    h1 = n1 // 2
    k1 = jnp.arange(n1, dtype=jnp.int32)
    m = (jnp.arange(n2, dtype=jnp.int32)[:, None, None] * k1[None, :, None]
         + n2 * k1[None, :, None] * jnp.arange(h1, dtype=jnp.int32)[None, None, :]) % N
    ang = (2.0 * math.pi / N) * m.astype(F32)
    c, s = jnp.cos(ang), jnp.sin(ang)
    block = lambda re, im: jnp.concatenate([jnp.concatenate([re, -im], -1), jnp.concatenate([im, re], -1)], -2)
    g = block(c, -s).astype(BF16)
    ct_, st_ = jnp.swapaxes(c, 1, 2) / N, jnp.swapaxes(s, 1, 2) / N
    gi = block(ct_, st_).astype(BF16)
    idx = jnp.arange(n2, dtype=jnp.int32)
    ang2 = (2.0 * math.pi / n2) * ((idx[:, None] * idx[None, :]) % n2).astype(F32)
    c2, s2 = jnp.cos(ang2), jnp.sin(ang2)
    return dict(n1=n1, n2=n2, h1=h1, g=g, gi=gi, f=block(c2, -s2).astype(BF16), fi=block(c2, s2).astype(BF16))


def _filter_spectrum_kernel(hf_ref, hb_ref, g_ref, f_ref, sum_ref, o_ref, z_ref, sa_ref, *, sa, n1, n2):
    step = pl.program_id(1)
    ng = n2 // FFT_GROUP

    @pl.when(step < sa)
    def _():
        _stage_a([hf_ref[...], hb_ref[...]], g_ref, z_ref, sa_ref, step, n1=n1, ng=ng, mirror_second=True)

    @pl.when(step >= sa)
    def _():
        scale = 1.0 / (sum_ref[...] + EPS)
        for j in range(FFT_K1_GROUP):
            k1 = (step - sa) * FFT_K1_GROUP + j
            zk = z_ref[k1].reshape(2 * n2, z_ref.shape[-1])
            o_ref[j] = (jnp.dot(f_ref[...], zk, preferred_element_type=F32) * scale).astype(o_ref.dtype)


def _filter_tables(tab):
    n1, n2, h1 = tab["n1"], tab["n2"], tab["h1"]
    N = n1 * n2
    k1 = jnp.arange(n1, dtype=jnp.int32)[None, :, None]
    i2 = jnp.arange(n2, dtype=jnp.int32)[:, None, None]
    a = jnp.arange(h1, dtype=jnp.int32)[None, None, :]
    rows = jnp.concatenate([a, n1 - 1 - a], axis=-1)
    m = (i2 * k1 + n2 * k1 * rows) % N
    ang = (2.0 * math.pi / N) * m.astype(F32)
    col = jnp.arange(2 * h1, dtype=jnp.int32)[None, None, :]
    live = jnp.logical_not(jnp.logical_and(i2 == 0, col == 2 * h1 - 1))
    re = jnp.where(live, jnp.cos(ang), 0.0)
    im = jnp.where(live, -jnp.sin(ang), 0.0)
    return jnp.concatenate([re, im], axis=1).astype(BF16)


def _filter_spectrum(hf, hb, abs_sum, tab):
    L, D = hf.shape
    n1, n2, h1 = tab["n1"], tab["n2"], tab["h1"]
    ct = FFT_LANES
    sa = n2 // FFT_GROUP
    sb = n1 // FFT_K1_GROUP
    kern = functools.partial(_filter_spectrum_kernel, sa=sa, n1=n1, n2=n2)
    return pl.pallas_call(
        kern,
        out_shape=jax.ShapeDtypeStruct((n1, 2 * n2, D), BF16),
        grid=(D // ct, sa + sb),
        in_specs=[
            pl.BlockSpec((h1, FFT_GROUP, ct), lambda t, s: (0, jnp.minimum(s, sa - 1), t)),
            pl.BlockSpec((h1, FFT_GROUP, ct), lambda t, s: (0, sa - 1 - jnp.minimum(s, sa - 1), t)),
            pl.BlockSpec((FFT_GROUP, 2 * n1, 2 * h1), lambda t, s: (jnp.minimum(s, sa - 1), 0, 0)),
            pl.BlockSpec((2 * n2, 2 * n2), lambda t, s: (0, 0)),
            pl.BlockSpec((1, ct), lambda t, s: (0, t)),
        ],
        out_specs=pl.BlockSpec((FFT_K1_GROUP, 2 * n2, ct), lambda t, s: (jnp.maximum(s - sa, 0), 0, t)),
        scratch_shapes=[pltpu.VMEM((n1, 2 * n2 // FFT_GROUP, FFT_GROUP, ct), BF16),
                        pltpu.VMEM((FFT_GROUP, 2 * n1, ct), F32)],
        compiler_params=_cparams("parallel", "arbitrary"),
        name="hy_filter_spectrum",
    )(hf.reshape(h1, n2, D), hb.reshape(h1, n2, D), tab["gf"], tab["f"], abs_sum)


def _fft_conv(vv, x0, kf, skip, tab):
    B, L, D = vv.shape
    n1, n2, h1 = tab["n1"], tab["n2"], tab["h1"]
    assert B % 2 == 0 and L == h1 * n2
    ct = FFT_LANES
    sa = n2 // FFT_GROUP
    sb = n1 // FFT_K1_GROUP
    sc = sa
    shape5 = (B // 2, 2, h1, n2, D)
    a_idx = lambda s: jnp.where(s < sa, s, jnp.where(s >= sa + sb, s - sa - sb, sa - 1))
    c_idx = lambda s: jnp.clip(s - sa - sb, 0, sc - 1)
    blk = (None, 2, h1, FFT_GROUP, ct)
    in_specs = [
        pl.BlockSpec(blk, lambda p, t, s: (p, 0, 0, a_idx(s), t)),
        pl.BlockSpec(blk, lambda p, t, s: (p, 0, 0, c_idx(s), t)),
        pl.BlockSpec((FFT_GROUP, 2 * n1, 2 * h1), lambda p, t, s: (jnp.minimum(s, sa - 1), 0, 0)),
        pl.BlockSpec((2 * n2, 2 * n2), lambda p, t, s: (0, 0)),
        pl.BlockSpec((2 * n2, 2 * n2), lambda p, t, s: (0, 0)),
        pl.BlockSpec((FFT_K1_GROUP, 2 * n2, ct), lambda p, t, s: (jnp.clip(s - sa, 0, sb - 1), 0, t)),
        pl.BlockSpec((FFT_GROUP, 2 * h1, 2 * n1), lambda p, t, s: (c_idx(s), 0, 0)),
        pl.BlockSpec((1, ct), lambda p, t, s: (0, t)),
    ]
    kern = functools.partial(_fft_conv_kernel, sa=sa, sb=sb, n1=n1, n2=n2, h1=h1)
    y = pl.pallas_call(
        kern,
        out_shape=jax.ShapeDtypeStruct(shape5, BF16),
        grid=(B // 2, D // ct, sa + sb + sc),
        in_specs=in_specs,
        out_specs=pl.BlockSpec(blk, lambda p, t, s: (p, 0, 0, c_idx(s), t)),
        scratch_shapes=[pltpu.VMEM((n1, 2 * n2 // FFT_GROUP, FFT_GROUP, ct), BF16),
                        pltpu.VMEM((FFT_GROUP, 2 * n1, ct), F32),
                        pltpu.VMEM((FFT_GROUP, 2 * h1, ct), F32)],
        compiler_params=_cparams("parallel", "parallel", "arbitrary"),
        name="hy_fft_conv",
    )(vv.reshape(shape5), x0.reshape(shape5), tab["g"], tab["f"], tab["fi"], kf, tab["gi"],
      skip.reshape(1, D))
    return y.reshape(B, L, D)


def _filter_taps_kernel(feat_ref, t_ref, wf_ref, wb_ref, dl_ref, hf_ref, hb_ref, sum_ref, *, rows, total_rows):
    i = pl.program_id(1)
    window = jnp.exp(-t_ref[...] * dl_ref[...]) + HY_SHIFT
    feat = feat_ref[...]
    hf = jnp.dot(feat, wf_ref[...], preferred_element_type=F32, precision=lax.Precision.HIGHEST) * window
    hb = jnp.dot(feat, wb_ref[...], preferred_element_type=F32, precision=lax.Precision.HIGHEST) * window
    hf_ref[...] = hf
    hb_ref[...] = hb
    row = lax.broadcasted_iota(jnp.int32, hb.shape, 0) + i * rows
    part = (jnp.sum(jnp.abs(hf), axis=0, keepdims=True)
            + jnp.sum(jnp.where(row < total_rows - 1, jnp.abs(hb), 0.0), axis=0, keepdims=True))

    @pl.when(i == 0)
    def _():
        sum_ref[...] = part

    @pl.when(i > 0)
    def _():
        sum_ref[...] += part


def _filter_taps(L, w1, b1, w2, b2, w3, b3, w_out, freq):
    hp = lax.Precision.HIGHEST
    D = w_out.shape[-1] // 2
    t = jnp.linspace(0.0, 1.0, L, dtype=F32)[:, None]
    w = (2.0 * math.pi / L) * jnp.arange(L, dtype=F32)[:, None]
    bands = jnp.linspace(1e-4, HY_BANDS - 1, HY_BANDS, dtype=F32)[None, :]
    z = jnp.concatenate([t, jnp.cos(bands * w), -jnp.sin(bands * w)], axis=-1)
    h = jnp.sin(freq * (jnp.dot(z, w1, precision=hp) + b1))
    h = jnp.sin(freq * (jnp.dot(h, w2, precision=hp) + b2))
    h = jnp.sin(freq * (jnp.dot(h, w3, precision=hp) + b3))
    max_decay = math.log(HY_TARGET) / HY_FAST_DECAY
    min_decay = math.log(HY_TARGET) / HY_SLOW_DECAY
    deltas = jnp.abs(jnp.linspace(min_decay, max_decay, D, dtype=F32))[None, :]
    rows, ct = min(512, L), 512
    width = h.shape[-1]
    nct = D // ct
    tap_spec = pl.BlockSpec((rows, ct), lambda j, i: (i, j))
    return pl.pallas_call(
        functools.partial(_filter_taps_kernel, rows=rows, total_rows=L),
        out_shape=(jax.ShapeDtypeStruct((L, D), F32), jax.ShapeDtypeStruct((L, D), F32),
                   jax.ShapeDtypeStruct((1, D), F32)),
        grid=(nct, L // rows),
        in_specs=[pl.BlockSpec((rows, width), lambda j, i: (i, 0)),
                  pl.BlockSpec((rows, 1), lambda j, i: (i, 0)),
                  pl.BlockSpec((width, ct), lambda j, i: (0, j)),
                  pl.BlockSpec((width, ct), lambda j, i: (0, nct + j)),
                  pl.BlockSpec((1, ct), lambda j, i: (0, j))],
        out_specs=(tap_spec, tap_spec, pl.BlockSpec((1, ct), lambda j, i: (0, j))),
        compiler_params=_cparams("parallel", "arbitrary"),
        name="hy_filter_taps",
    )(h, t, w_out, w_out, deltas)


def _hyena_mixer(h, x, g1, p, tab):
    B, L, D = x.shape
    z = _matmul(h.reshape(B * L, D), p["w_in"], layer=p["slot"], bias=p["b_in"], out_dtype=BF16, name="hy_in").reshape(B, L, 3 * D)
    x0, vv = _conv_gate(z, p["conv_w"], p["conv_b"])
    hf, hb, abs_sum = _filter_taps(L, p["f_w1"], p["f_b1"], p["f_w2"], p["f_b2"], p["f_w3"], p["f_b3"],
                                   p["f_wout"], p["freq"])
    y = _fft_conv(vv, x0, _filter_spectrum(hf, hb, abs_sum, tab), p["skip"], tab)
    out = _matmul(y.reshape(B * L, D), p["w_out"], layer=p["slot"], bias=p["b_out"],
                  resid=x.reshape(B * L, D), gate=g1, rows_per_gate=L, name="hy_out")
    return out.reshape(B, L, D)


STACKED_WEIGHTS = ("w_in", "w_out", "gate_w1")


def _layer_params(params, slot):
    p = {n: (v if n in STACKED_WEIGHTS else v[slot]) for n, v in params.items()}
    p["slot"] = slot
    return p


def _trunk(x, mod, hy, ret, gla, norm_g, mlp_w1, mlp_w2, final_g):
    B, L, D = x.shape
    tab = _dft_tables(L)
    tab["gf"] = _filter_tables(tab)
    for i in range(DEPTH):
        sh1, sc1, g1, sh2, sc2, g2 = jnp.split(mod[i], 6, axis=-1)
        h = _norm(x, norm_g[i, 0], sc1, sh1, out_dtype=BF16, name="norm_mix")
        kind, slot = i % N_MIXERS, i // N_MIXERS
        if kind == 0:
            x = _hyena_mixer(h, x, g1, _layer_params(hy, slot), tab)
        elif kind == 1:
            x = _retention_mixer(h, x, g1, _layer_params(ret, slot))
        else:
            x = _gla_mixer(h, x, g1, _layer_params(gla, slot))
        h = _norm(x, norm_g[i, 1], sc2, sh2, out_dtype=BF16, name="norm_mlp")
        a = _matmul(h.reshape(B * L, D), mlp_w1, layer=i, act="relu2", out_dtype=BF16, name="mlp_up")
        x = _matmul_deep(a, mlp_w2, i, x.reshape(B * L, D), g2, L, name="mlp_down").reshape(B, L, D)
    return _norm(x, final_g, out_dtype=F32, name="norm_final")


def kernel(x_prompt, x_sample, c_prompt, c_sample, hy_w_in, hy_b_in, hy_conv_w, hy_conv_b, hy_f_w1, hy_f_b1, hy_f_w2, hy_f_b2, hy_f_w3, hy_f_b3, hy_f_wout, hy_freq, hy_skip, hy_w_out, hy_b_out, ret_w_in, ret_w_out, gla_w_in, gla_gate_w1, gla_gate_w2, gla_gate_b, gla_w_out, norm_g, ada_w, ada_b, mlp_w1, mlp_w2, final_g):
    D = x_prompt.shape[-1]
    bf = lambda w: w.astype(BF16)
    hy = dict(w_in=bf(hy_w_in), b_in=hy_b_in, conv_w=hy_conv_w, conv_b=hy_conv_b,
              f_w1=hy_f_w1, f_b1=hy_f_b1, f_w2=hy_f_w2, f_b2=hy_f_b2, f_w3=hy_f_w3, f_b3=hy_f_b3,
              f_wout=hy_f_wout, freq=hy_freq, skip=hy_skip, w_out=bf(hy_w_out), b_out=hy_b_out)

    perm = _pair_split_perm(D, RET_HEADS)
    cols = np.concatenate([perm, D + perm, np.arange(2 * D, 4 * D)])
    ret = dict(w_in=bf(ret_w_in[:, :, cols]), w_out=bf(ret_w_out))

    n_gla, _, _, rank = gla_gate_w1.shape
    w1 = jnp.concatenate([gla_gate_w1[:, 0], gla_gate_w1[:, 1],
                          jnp.zeros((n_gla, D, V7X_LANES - 2 * rank), F32)], axis=-1)
    w2 = jnp.zeros((n_gla, 2, V7X_LANES, gla_gate_w2.shape[-1]), F32)
    w2 = w2.at[:, 0, :rank].set(gla_gate_w2[:, 0]).at[:, 1, rank:2 * rank].set(gla_gate_w2[:, 1])
    gla = dict(w_in=bf(gla_w_in), gate_w1=bf(w1), gate_w2=w2, gate_b=gla_gate_b[:, :, None, :],
               w_out=bf(gla_w_out))
    mw1, mw2 = bf(mlp_w1), bf(mlp_w2)

    nb_p, nb_s = c_prompt.shape[0], c_sample.shape[0]
    cs = jax.nn.silu(jnp.concatenate([c_prompt, c_sample], axis=0))
    rows = -(-cs.shape[0] // 16) * 16
    cs = jnp.pad(cs, ((0, rows - cs.shape[0]), (0, 0))).astype(BF16)
    mod = jnp.stack([_matmul(cs, ada_w, layer=i, bias=ada_b[i], bn=512, name="ada") for i in range(DEPTH)])
    mod_p, mod_s = mod[:, :nb_p], mod[:, nb_p:nb_p + nb_s]

    y_prompt = _trunk(x_prompt, mod_p, hy, ret, gla, norm_g, mw1, mw2, final_g)
    y_sample = _trunk(x_sample, mod_s, hy, ret, gla, norm_g, mw1, mw2, final_g)
    return (y_prompt, y_sample)
```

```python
import functools
import math

import jax
import jax.numpy as jnp
import numpy as np
from jax import lax
from jax.experimental import pallas as pl
from jax.experimental.pallas import tpu as pltpu

D_MODEL = 4096
DEPTH = 4
N_MIXERS = 3
EPS = 1e-6
HY_BANDS = 16
HY_FAST_DECAY = 0.3
HY_SLOW_DECAY = 1.5
HY_TARGET = 1e-2
HY_SHIFT = 0.05
RET_HEADS = 16
ROPE_BASE = 10000.0
GLA_HEADS = 4
GLA_GATE_RANK = 16
GLA_TAU = 16.0

F32 = jnp.float32
BF16 = jnp.bfloat16

V7X_VMEM_LIMIT_BYTES = 56 * 1024 * 1024
V7X_LANES = 128
MM_BLOCK_M = 1024
MM_BLOCK_N = 1024
MM_BLOCK_K = 4096
MLP_UP_TILES = (dict(), dict(bm=2048, bn=512), dict(bm=512, bn=2048), dict())
MLP_DOWN_TILES = (dict(), dict(bk=4096, bn=512), dict(bm=512, bk=4096), dict(bk=2048, bn=512))
HY_IN_TILES = (dict(), dict(bm=2048, bn=512))
NORM_BLOCK_ROWS = 512
RET_CHUNK = 256
RET_HEADS_PER_STEP = 4
GLA_CHUNK = 128
GLA_SUBBLOCK = 32
GLA_EXP_CLAMP = 80.0


def _cparams(*sem):
    return pltpu.CompilerParams(dimension_semantics=sem, vmem_limit_bytes=V7X_VMEM_LIMIT_BYTES)


def _mm_kernel(*refs, nk, has_bias, has_resid, act):
    it = iter(refs)
    x_ref, w_ref = next(it), next(it)
    b_ref = next(it) if has_bias else None
    r_ref = next(it) if has_resid else None
    g_ref = next(it) if has_resid else None
    o_ref = next(it)
    acc_ref = next(it) if nk > 1 else None

    part = jnp.dot(x_ref[...].astype(BF16), w_ref[...].astype(BF16), preferred_element_type=F32)

    def finish(acc):
        if has_bias:
            acc = acc + b_ref[...]
        if act == "relu2":
            acc = jnp.maximum(acc, 0.0)
            acc = acc * acc
        if has_resid:
            acc = r_ref[...] + g_ref[...] * acc
        o_ref[...] = acc.astype(o_ref.dtype)

    if nk == 1:
        finish(part)
    else:
        k = pl.program_id(2)

        @pl.when(k == 0)
        def _():
            acc_ref[...] = part

        @pl.when(jnp.logical_and(k > 0, k < nk - 1))
        def _():
            acc_ref[...] += part

        @pl.when(k == nk - 1)
        def _():
            finish(acc_ref[...] + part)


def _matmul(x, w, *, layer=None, bias=None, act=None, resid=None, gate=None, rows_per_gate=None,
            out_dtype=F32, bm=MM_BLOCK_M, bn=MM_BLOCK_N, bk=MM_BLOCK_K, name="matmul"):
    M, K = x.shape
    K2, N = w.shape[-2:]
    assert K == K2
    bm, bn, bk = min(bm, M, rows_per_gate or M), min(bn, N), min(bk, K)
    assert M % bm == 0 and N % bn == 0 and K % bk == 0
    nk = K // bk
    has_bias = bias is not None
    has_resid = resid is not None

    operands = [x, w]
    in_specs = [pl.BlockSpec((bm, bk), lambda i, j, k: (i, k)),
                pl.BlockSpec((bk, bn), lambda i, j, k: (k, j)) if layer is None else
                pl.BlockSpec((None, bk, bn), lambda i, j, k: (layer, k, j))]
    if has_bias:
        operands.append(bias.reshape(1, N).astype(F32))
        in_specs.append(pl.BlockSpec((1, bn), lambda i, j, k: (0, j)))
    if has_resid:
        assert rows_per_gate % bm == 0
        tiles_per_gate = rows_per_gate // bm
        operands.append(resid)
        in_specs.append(pl.BlockSpec((bm, bn), lambda i, j, k: (i, j)))
        operands.append(gate.reshape(gate.shape[0], 1, N).astype(F32))
        in_specs.append(pl.BlockSpec((None, 1, bn), lambda i, j, k: (i // tiles_per_gate, 0, j)))

    scratch = [pltpu.VMEM((bm, bn), F32)] if nk > 1 else []
    kern = functools.partial(_mm_kernel, nk=nk, has_bias=has_bias, has_resid=has_resid, act=act)
    return pl.pallas_call(
        kern,
        out_shape=jax.ShapeDtypeStruct((M, N), out_dtype),
        grid=(M // bm, N // bn, nk),
        in_specs=in_specs,
        out_specs=pl.BlockSpec((bm, bn), lambda i, j, k: (i, j)),
        scratch_shapes=scratch,
        compiler_params=_cparams("parallel", "parallel", "arbitrary"),
        name=name,
    )(*operands)


def _mm_deep_kernel(x_ref, w_ref, r_ref, g_ref, o_ref, acc_ref, *, nk, bn):
    k = pl.program_id(1)
    cols = pl.ds(pl.multiple_of(pl.program_id(2) * bn, bn), bn)
    part = jnp.dot(x_ref[...], w_ref[...], preferred_element_type=F32)

    @pl.when(k == 0)
    def _():
        acc_ref[:, cols] = part

    @pl.when(jnp.logical_and(k > 0, k < nk - 1))
    def _():
        acc_ref[:, cols] += part

    @pl.when(k == nk - 1)
    def _():
        o_ref[...] = r_ref[...] + g_ref[...] * (acc_ref[:, cols] + part)


def _matmul_deep(x, w, layer, resid, gate, rows_per_gate, *, bm=MM_BLOCK_M, bn=MM_BLOCK_N, bk=2048, name):
    M, K = x.shape
    N = w.shape[-1]
    bm = min(bm, M, rows_per_gate)
    assert M % bm == 0 and N % bn == 0 and K % bk == 0 and rows_per_gate % bm == 0
    nk = K // bk
    assert nk >= 2
    tiles_per_gate = rows_per_gate // bm
    jcol = lambda k, j: jnp.where(k == nk - 1, j, 0)
    return pl.pallas_call(
        functools.partial(_mm_deep_kernel, nk=nk, bn=bn),
        out_shape=jax.ShapeDtypeStruct((M, N), F32),
        grid=(M // bm, nk, N // bn),
        in_specs=[pl.BlockSpec((bm, bk), lambda i, k, j: (i, k)),
                  pl.BlockSpec((None, bk, bn), lambda i, k, j: (layer, k, j)),
                  pl.BlockSpec((bm, bn), lambda i, k, j: (i, jcol(k, j))),
                  pl.BlockSpec((None, 1, bn), lambda i, k, j: (i // tiles_per_gate, 0, jcol(k, j)))],
        out_specs=pl.BlockSpec((bm, bn), lambda i, k, j: (i, jcol(k, j))),
        scratch_shapes=[pltpu.VMEM((bm, N), F32)],
        compiler_params=_cparams("parallel", "arbitrary", "arbitrary"),
        name=name,
    )(x, w, resid, gate.reshape(gate.shape[0], 1, N).astype(F32))


def _norm_kernel(x_ref, g_ref, *rest, modulate):
    if modulate:
        sc_ref, sh_ref, o_ref = rest
    else:
        (o_ref,) = rest
    x = x_ref[...]
    y = x * lax.rsqrt(jnp.mean(x * x, axis=-1, keepdims=True) + EPS) * g_ref[...]
    if modulate:
        y = y * (1.0 + sc_ref[...]) + sh_ref[...]
    o_ref[...] = y.astype(o_ref.dtype)


def _norm(x, g, scale=None, shift=None, *, out_dtype, name):
    B, L, D = x.shape
    rows = min(NORM_BLOCK_ROWS, L)
    assert L % rows == 0
    modulate = scale is not None
    operands = [x, g.reshape(1, D)]
    in_specs = [pl.BlockSpec((None, rows, D), lambda b, i: (b, i, 0)),
                pl.BlockSpec((1, D), lambda b, i: (0, 0))]
    if modulate:
        operands += [scale.reshape(B, 1, D), shift.reshape(B, 1, D)]
        in_specs += [pl.BlockSpec((None, 1, D), lambda b, i: (b, 0, 0))] * 2
    return pl.pallas_call(
        functools.partial(_norm_kernel, modulate=modulate),
        out_shape=jax.ShapeDtypeStruct((B, L, D), out_dtype),
        grid=(B, L // rows),
        in_specs=in_specs,
        out_specs=pl.BlockSpec((None, rows, D), lambda b, i: (b, i, 0)),
        compiler_params=_cparams("parallel", "parallel"),
        name=name,
    )(*operands)


def _silu(x):
    return x / (1.0 + jnp.exp(-x))


def _gated_head_norm(o, gate):
    o = o * lax.rsqrt(jnp.mean(o * o, axis=-1, keepdims=True) + EPS)
    return _silu(gate) * o


def _ret_kernel(lg_ref, q_ref, k_ref, v_ref, cos_ref, sin_ref, *rest, reverse, final, heads_per_step, hd):
    if final:
        op_ref, g_ref, o_ref, state_ref = rest
    else:
        o_ref, state_ref = rest
    C = q_ref.shape[0]
    half = hd // 2

    @pl.when(pl.program_id(2) == 0)
    def _():
        state_ref[...] = jnp.zeros_like(state_ref)

    cos, sin = cos_ref[...], sin_ref[...]
    row = lax.broadcasted_iota(jnp.int32, (C, C), 0)
    col = lax.broadcasted_iota(jnp.int32, (C, C), 1)
    ridx = lax.broadcasted_iota(jnp.int32, (C, 1), 0).astype(F32)
    if reverse:
        mask = col > row
        dist = (col - row).astype(F32)
        q_pow, k_pow = C - ridx, ridx
    else:
        mask = col <= row
        dist = (row - col).astype(F32)
        q_pow, k_pow = ridx + 1.0, C - 1.0 - ridx
    dist = jnp.where(mask, dist, 0.0)

    def rotate(x):
        x1, x2 = x[:, :half], x[:, half:]
        return jnp.concatenate([x1 * cos - x2 * sin, x1 * sin + x2 * cos], axis=1)

    for hh in range(heads_per_step):
        lg = lg_ref[pl.program_id(1) * heads_per_step + hh]
        sl = slice(hh * hd, (hh + 1) * hd)
        q = rotate(q_ref[:, sl].astype(F32))
        k = rotate(k_ref[:, sl].astype(F32)) * (hd ** -0.5)
        v = v_ref[:, sl].astype(BF16)
        intra = jnp.where(mask, jnp.exp(lg * dist), 0.0)
        s = lax.dot_general(q.astype(BF16), k.astype(BF16), (((1,), (1,)), ((), ())),
                            preferred_element_type=F32) * intra
        state = state_ref[hh]
        o = (jnp.dot(s.astype(BF16), v, preferred_element_type=F32)
             + jnp.dot((q * jnp.exp(lg * q_pow)).astype(BF16), state.astype(BF16),
                       preferred_element_type=F32))
        kd = (k * jnp.exp(lg * k_pow)).astype(BF16)
        state_ref[hh] = state * jnp.exp(lg * jnp.full((1, hd), C, F32)) + lax.dot_general(
            kd, v, (((0,), (0,)), ((), ())), preferred_element_type=F32)
        if final:
            o_ref[:, sl] = _gated_head_norm(op_ref[:, sl].astype(F32) + o,
                                            g_ref[:, sl].astype(F32)).astype(o_ref.dtype)
        else:
            o_ref[:, sl] = o.astype(o_ref.dtype)


def _ret_pass(z, log_gamma, cos, sin, o_prev, *, reverse):
    B, L, D4 = z.shape
    D = D4 // 4
    hd = D // RET_HEADS
    hps = RET_HEADS_PER_STEP
    W = hps * hd
    C = min(RET_CHUNK, L)
    n = L // C
    nblk = D // W
    final = o_prev is not None
    cidx = (lambda c: n - 1 - c) if reverse else (lambda c: c)

    def zspec(part):
        return pl.BlockSpec((None, C, W), lambda b, h, c, lg: (b, cidx(c), part * nblk + h))

    tspec = pl.BlockSpec((C, hd // 2), lambda b, h, c, lg: (cidx(c), 0))
    ospec = pl.BlockSpec((None, C, W), lambda b, h, c, lg: (b, cidx(c), h))
    operands = [z, z, z, cos, sin]
    in_specs = [zspec(0), zspec(1), zspec(2), tspec, tspec]
    if final:
        operands += [o_prev, z]
        in_specs += [ospec, zspec(3)]
    kern = functools.partial(_ret_kernel, reverse=reverse, final=final, heads_per_step=hps, hd=hd)
    return pl.pallas_call(
        kern,
        out_shape=jax.ShapeDtypeStruct((B, L, D), BF16),
        grid_spec=pltpu.PrefetchScalarGridSpec(
            num_scalar_prefetch=1,
            grid=(B, nblk, n),
            in_specs=in_specs,
            out_specs=ospec,
            scratch_shapes=[pltpu.VMEM((hps, hd, hd), F32)]),
        compiler_params=_cparams("parallel", "parallel", "arbitrary"),
        name="ret_bwd" if reverse else "ret_fwd",
    )(log_gamma, *operands)


def _pair_split_perm(d, heads):
    hd = d // heads
    idx = np.arange(d).reshape(heads, hd)
    return np.concatenate([idx[:, 0::2], idx[:, 1::2]], axis=1).reshape(-1)


def _retention_mixer(h, x, g1, p):
    B, L, D = x.shape
    hd = D // RET_HEADS
    z = _matmul(h.reshape(B * L, D), p["w_in"], layer=p["slot"], out_dtype=BF16,
                name="ret_in").reshape(B, L, 4 * D)
    pos = jnp.arange(L, dtype=F32)
    inv = 1.0 / (ROPE_BASE ** jnp.linspace(0.0, 1.0, hd // 2, dtype=F32))
    ang = pos[:, None] * inv[None, :]
    cos, sin = jnp.cos(ang), jnp.sin(ang)
    hidx = jnp.arange(RET_HEADS, dtype=F32)
    lg_fwd = jnp.log1p(-jnp.exp2(-5.0 - hidx))
    lg_bwd = jnp.log1p(-jnp.exp2(-5.0 - hidx[::-1]))
    o_fwd = _ret_pass(z, lg_fwd, cos, sin, None, reverse=False)
    y = _ret_pass(z, lg_bwd, cos, sin, o_fwd, reverse=True)
    out = _matmul(y.reshape(B * L, D), p["w_out"], layer=p["slot"], resid=x.reshape(B * L, D), gate=g1,
                  rows_per_gate=L, name="ret_out")
    return out.reshape(B, L, D)


def _gla_kernel(q_ref, k_ref, v_ref, t_ref, w2_ref, gb_ref, *rest, reverse, final, sub):
    if final:
        op_ref, r_ref, o_ref, state_ref, a_ref = rest
    else:
        o_ref, state_ref, a_ref = rest
    C, dk = q_ref.shape

    @pl.when(pl.program_id(2) == 0)
    def _():
        state_ref[...] = jnp.zeros_like(state_ref)

    zg = jnp.dot(t_ref[...].astype(BF16), w2_ref[...].astype(BF16), preferred_element_type=F32) + gb_ref[...]
    log_a = (jnp.minimum(zg, 0.0) - jnp.log1p(jnp.exp(-jnp.abs(zg)))) * (1.0 / GLA_TAU)

    row = lax.broadcasted_iota(jnp.int32, (C, C), 0)
    col = lax.broadcasted_iota(jnp.int32, (C, C), 1)
    tri = ((col >= row) if reverse else (col <= row)).astype(BF16)
    hi = log_a.astype(BF16)
    lo = (log_a - hi.astype(F32)).astype(BF16)
    b = (jnp.dot(tri, hi, preferred_element_type=F32) + jnp.dot(tri, lo, preferred_element_type=F32))
    b_tot = jnp.sum(log_a, axis=0, keepdims=True)

    q = q_ref[...].astype(F32) * (dk ** -0.5)
    k = k_ref[...].astype(F32)
    v = v_ref[...].astype(BF16)

    for blk in range(C // sub):
        rows = slice(blk * sub, (blk + 1) * sub)
        ref = b[blk * sub + sub // 2:blk * sub + sub // 2 + 1, :]
        qh = q[rows] * jnp.exp(jnp.minimum(b[rows] - ref, GLA_EXP_CLAMP))
        kh = k * jnp.exp(jnp.minimum(ref - b, GLA_EXP_CLAMP))
        s = lax.dot_general(qh.astype(BF16), kh.astype(BF16), (((1,), (1,)), ((), ())),
                            preferred_element_type=F32)
        r_i = lax.broadcasted_iota(jnp.int32, (sub, C), 0) + blk * sub
        c_i = lax.broadcasted_iota(jnp.int32, (sub, C), 1)
        keep = (c_i > r_i) if reverse else (c_i <= r_i)
        a_ref[rows, :] = jnp.where(keep, s, 0.0).astype(BF16)

    state = state_ref[...]
    o = (jnp.dot(a_ref[...], v, preferred_element_type=F32)
         + lax.dot_general((q * jnp.exp(b)).astype(BF16), state.astype(BF16), (((1,), (1,)), ((), ())),
                           preferred_element_type=F32))
    kd = (k * jnp.exp(b_tot - b)).astype(BF16)
    state_ref[...] = state * jnp.exp(b_tot) + lax.dot_general(
        v, kd, (((0,), (0,)), ((), ())), preferred_element_type=F32)
    if final:
        o_ref[...] = _gated_head_norm(op_ref[...].astype(F32) + o, r_ref[...].astype(F32)).astype(o_ref.dtype)
    else:
        o_ref[...] = o.astype(o_ref.dtype)


def _gla_pass(z, t, w2, gb, o_prev, *, direction):
    B, L, D3 = z.shape
    D = D3 // 3
    dk = D // 2 // GLA_HEADS
    dv = D // GLA_HEADS
    C = min(GLA_CHUNK, L)
    n = L // C
    reverse = direction == 1
    final = o_prev is not None
    cidx = (lambda c: n - 1 - c) if reverse else (lambda c: c)
    H = GLA_HEADS

    qspec = pl.BlockSpec((None, C, dk), lambda b, h, c: (b, cidx(c), h))
    kspec = pl.BlockSpec((None, C, dk), lambda b, h, c: (b, cidx(c), H + h))
    vspec = pl.BlockSpec((None, C, dv), lambda b, h, c: (b, cidx(c), H + h))
    rspec = pl.BlockSpec((None, C, dv), lambda b, h, c: (b, cidx(c), 2 * H + h))
    ospec = pl.BlockSpec((None, C, dv), lambda b, h, c: (b, cidx(c), h))
    tspec = pl.BlockSpec((None, C, t.shape[-1]), lambda b, h, c: (b, cidx(c), 0))
    w2spec = pl.BlockSpec((None, t.shape[-1], dk), lambda b, h, c: (direction, 0, h))
    gbspec = pl.BlockSpec((None, 1, dk), lambda b, h, c: (direction, 0, h))
    operands = [z, z, z, t, w2, gb]
    in_specs = [qspec, kspec, vspec, tspec, w2spec, gbspec]
    if final:
        operands += [o_prev, z]
        in_specs += [ospec, rspec]
    kern = functools.partial(_gla_kernel, reverse=reverse, final=final, sub=min(GLA_SUBBLOCK, C))
    return pl.pallas_call(
        kern,
        out_shape=jax.ShapeDtypeStruct((B, L, D), BF16),
        grid=(B, H, n),
        in_specs=in_specs,
        out_specs=ospec,
        scratch_shapes=[pltpu.VMEM((dv, dk), F32), pltpu.VMEM((C, C), BF16)],
        compiler_params=_cparams("parallel", "parallel", "arbitrary"),
        name="gla_bwd" if reverse else "gla_fwd",
    )(*operands)


def _gla_mixer(h, x, g1, p):
    B, L, D = x.shape
    h2 = h.reshape(B * L, D)
    z = _matmul(h2, p["w_in"], layer=p["slot"], out_dtype=BF16, name="gla_in").reshape(B, L, 3 * D)
    t = _matmul(h2, p["gate_w1"], layer=p["slot"], bn=V7X_LANES, name="gla_gate").reshape(B, L, V7X_LANES)
    o_fwd = _gla_pass(z, t, p["gate_w2"], p["gate_b"], None, direction=0)
    y = _gla_pass(z, t, p["gate_w2"], p["gate_b"], o_fwd, direction=1)
    out = _matmul(y.reshape(B * L, D), p["w_out"], layer=p["slot"], resid=x.reshape(B * L, D), gate=g1,
                  rows_per_gate=L, name="gla_out")
    return out.reshape(B, L, D)


CONV_HALO = 16


def _conv_gate_kernel(*refs, rows):
    groups = [refs[5 * g:5 * g + 5] for g in range(3)]
    x0_out, vv_out = refs[15:]
    i = pl.program_id(1)
    first = i == 0
    last = i == pl.num_programs(1) - 1
    h = CONV_HALO
    ct = groups[0][0].shape[1]
    row = lax.broadcasted_iota(jnp.int32, (rows, ct), 0)

    def put(out_ref, val):
        lanes = out_ref.shape[-1]
        for t in range(out_ref.shape[0]):
            out_ref[t] = val[:, t * lanes:(t + 1) * lanes].astype(out_ref.dtype)

    def conv(cur_ref, prev_ref, next_ref, w_ref, b_ref):
        cur = cur_ref[...].astype(F32)
        before = jnp.where(first, 0.0, prev_ref[h - 1:h, :].astype(F32))
        after = jnp.where(last, 0.0, next_ref[0:1, :].astype(F32))
        prev = jnp.where(row == 0, before, pltpu.roll(cur, 1, axis=0))
        nxt = jnp.where(row == rows - 1, after, pltpu.roll(cur, rows - 1, axis=0))
        w = w_ref[...]
        return prev * w[0:1] + cur * w[1:2] + nxt * w[2:3] + b_ref[...]

    put(x0_out, conv(*groups[0]))
    x1 = conv(*groups[1])
    put(vv_out, conv(*groups[2]) * x1)


def _conv_gate(z, conv_w, conv_b):
    B, L, D3 = z.shape
    D = D3 // 3
    rows, ct = min(512, L), 512
    nct = D // ct
    h = CONV_HALO
    nhalo = L // h
    operands, in_specs = [], []
    for g in range(3):
        operands += [z, z, z, conv_w, conv_b.reshape(1, D3)]
        in_specs += [
            pl.BlockSpec((None, rows, ct), lambda b, i, j, g=g: (b, i, g * nct + j)),
            pl.BlockSpec((None, h, ct), lambda b, i, j, g=g: (b, jnp.maximum(i * (rows // h) - 1, 0), g * nct + j)),
            pl.BlockSpec((None, h, ct), lambda b, i, j, g=g: (b, jnp.minimum((i + 1) * (rows // h), nhalo - 1), g * nct + j)),
            pl.BlockSpec((3, ct), lambda b, i, j, g=g: (0, g * nct + j)),
            pl.BlockSpec((1, ct), lambda b, i, j, g=g: (0, g * nct + j)),
        ]
    tiles = ct // FFT_LANES
    ospec = pl.BlockSpec((None, tiles, rows, FFT_LANES), lambda b, i, j: (b, j, i, 0))
    oshape = jax.ShapeDtypeStruct((B, D // FFT_LANES, L, FFT_LANES), BF16)
    return pl.pallas_call(
        functools.partial(_conv_gate_kernel, rows=rows),
        out_shape=(oshape, oshape),
        grid=(B, L // rows, nct),
        in_specs=in_specs,
        out_specs=(ospec, ospec),
        compiler_params=_cparams("parallel", "parallel", "parallel"),
        name="hy_conv_gate",
    )(*operands)


FFT_GROUP = 16
FFT_K1_GROUP = 8
FFT_LANES = 256


def _stage_a(x_parts, g_ref, z_ref, stage_ref, step, *, n1, ng, mirror_second=False):
    xt = [jnp.swapaxes(x, 0, 1) for x in x_parts]
    for j in range(FFT_GROUP):
        x = jnp.concatenate([xt[0][j], xt[1][FFT_GROUP - 1 - j if mirror_second else j]], axis=0)
        stage_ref[j] = jnp.dot(g_ref[j], x.astype(BF16), preferred_element_type=F32)
    a = jnp.swapaxes(stage_ref[...], 0, 1)
    z_ref[:, step] = a[:n1].astype(z_ref.dtype)
    z_ref[:, ng + step] = a[n1:].astype(z_ref.dtype)


def _fft_conv_kernel(x_ref, x0_ref, g_ref, f_ref, fi_ref, kf_ref, gi_ref, skip_ref, o_ref, z_ref, sa_ref, sc_ref,
                     *, sa, sb, n1, n2, h1):
    step = pl.program_id(2)
    ng = n2 // FFT_GROUP

    @pl.when(step < sa)
    def _():
        _stage_a([x_ref[0].astype(F32), x_ref[1].astype(F32)], g_ref, z_ref, sa_ref, step, n1=n1, ng=ng)

    @pl.when(jnp.logical_and(step >= sa, step < sa + sb))
    def _():
        for j in range(FFT_K1_GROUP):
            k1 = (step - sa) * FFT_K1_GROUP + j
            zk = z_ref[k1].reshape(2 * n2, z_ref.shape[-1])
            y = jnp.dot(f_ref[...], zk, preferred_element_type=F32)
            yr, yi = y[:n2], y[n2:]
            kr, ki = kf_ref[j, :n2, :], kf_ref[j, n2:, :]
            p = jnp.concatenate([yr * kr - yi * ki, yr * ki + yi * kr], axis=0).astype(BF16)
            back = jnp.dot(fi_ref[...], p, preferred_element_type=F32)
            z_ref[k1] = back.astype(z_ref.dtype).reshape(z_ref.shape[1:])

    @pl.when(step >= sa + sb)
    def _():
        g = step - sa - sb
        zr = jnp.swapaxes(z_ref[:, g].astype(F32), 0, 1)
        zi = jnp.swapaxes(z_ref[:, ng + g].astype(F32), 0, 1)
        for j in range(FFT_GROUP):
            bz = jnp.concatenate([zr[j], zi[j]], axis=0).astype(BF16)
            sc_ref[j] = jnp.dot(gi_ref[j], bz, preferred_element_type=F32)
        y = jnp.swapaxes(sc_ref[...], 0, 1).reshape(x_ref.shape)
        o_ref[...] = ((y + skip_ref[...] * x_ref[...].astype(F32)) * x0_ref[...].astype(F32)).astype(o_ref.dtype)


def _dft_tables(L):
    N = 2 * L
    n1 = n2 = int(round(math.sqrt(N)))
    assert n1 * n2 == N and n2 % FFT_GROUP == 0 and n1 % FFT_K1_GROUP == 0
    h1 = n1 // 2
    k1 = jnp.arange(n1, dtype=jnp.int32)
    m = (jnp.arange(n2, dtype=jnp.int32)[:, None, None] * k1[None, :, None]
         + n2 * k1[None, :, None] * jnp.arange(h1, dtype=jnp.int32)[None, None, :]) % N
    ang = (2.0 * math.pi / N) * m.astype(F32)
    c, s = jnp.cos(ang), jnp.sin(ang)
    block = lambda re, im: jnp.concatenate([jnp.concatenate([re, -im], -1), jnp.concatenate([im, re], -1)], -2)
    g = block(c, -s).astype(BF16)
    ct_, st_ = jnp.swapaxes(c, 1, 2) / N, jnp.swapaxes(s, 1, 2) / N
    gi = block(ct_, st_).astype(BF16)
    idx = jnp.arange(n2, dtype=jnp.int32)
    ang2 = (2.0 * math.pi / n2) * ((idx[:, None] * idx[None, :]) % n2).astype(F32)
    c2, s2 = jnp.cos(ang2), jnp.sin(ang2)
    return dict(n1=n1, n2=n2, h1=h1, g=g, gi=gi, f=block(c2, -s2).astype(BF16), fi=block(c2, s2).astype(BF16))


def _filter_spectrum_kernel(hf_ref, hb_ref, g_ref, f_ref, sum_ref, o_ref, z_ref, sa_ref, *, sa, n1, n2):
    step = pl.program_id(1)
    ng = n2 // FFT_GROUP

    @pl.when(step < sa)
    def _():
        _stage_a([hf_ref[...], hb_ref[...]], g_ref, z_ref, sa_ref, step, n1=n1, ng=ng, mirror_second=True)

    @pl.when(step >= sa)
    def _():
        scale = 1.0 / (sum_ref[...] + EPS)
        for j in range(FFT_K1_GROUP):
            k1 = (step - sa) * FFT_K1_GROUP + j
            zk = z_ref[k1].reshape(2 * n2, z_ref.shape[-1])
            o_ref[j] = (jnp.dot(f_ref[...], zk, preferred_element_type=F32) * scale).astype(o_ref.dtype)


def _filter_tables(tab):
    n1, n2, h1 = tab["n1"], tab["n2"], tab["h1"]
    N = n1 * n2
    k1 = jnp.arange(n1, dtype=jnp.int32)[None, :, None]
    i2 = jnp.arange(n2, dtype=jnp.int32)[:, None, None]
    a = jnp.arange(h1, dtype=jnp.int32)[None, None, :]
    rows = jnp.concatenate([a, n1 - 1 - a], axis=-1)
    m = (i2 * k1 + n2 * k1 * rows) % N
    ang = (2.0 * math.pi / N) * m.astype(F32)
    col = jnp.arange(2 * h1, dtype=jnp.int32)[None, None, :]
    live = jnp.logical_not(jnp.logical_and(i2 == 0, col == 2 * h1 - 1))
    re = jnp.where(live, jnp.cos(ang), 0.0)
    im = jnp.where(live, -jnp.sin(ang), 0.0)
    return jnp.concatenate([re, im], axis=1).astype(BF16)


def _filter_spectrum(hf, hb, abs_sum, tab):
    L, D = hf.shape
    n1, n2, h1 = tab["n1"], tab["n2"], tab["h1"]
    ct = FFT_LANES
    sa = n2 // FFT_GROUP
    sb = n1 // FFT_K1_GROUP
    kern = functools.partial(_filter_spectrum_kernel, sa=sa, n1=n1, n2=n2)
    return pl.pallas_call(
        kern,
        out_shape=jax.ShapeDtypeStruct((n1, 2 * n2, D), BF16),
        grid=(D // ct, sa + sb),
        in_specs=[
            pl.BlockSpec((h1, FFT_GROUP, ct), lambda t, s: (0, jnp.minimum(s, sa - 1), t)),
            pl.BlockSpec((h1, FFT_GROUP, ct), lambda t, s: (0, sa - 1 - jnp.minimum(s, sa - 1), t)),
            pl.BlockSpec((FFT_GROUP, 2 * n1, 2 * h1), lambda t, s: (jnp.minimum(s, sa - 1), 0, 0)),
            pl.BlockSpec((2 * n2, 2 * n2), lambda t, s: (0, 0)),
            pl.BlockSpec((1, ct), lambda t, s: (0, t)),
        ],
        out_specs=pl.BlockSpec((FFT_K1_GROUP, 2 * n2, ct), lambda t, s: (jnp.maximum(s - sa, 0), 0, t)),
        scratch_shapes=[pltpu.VMEM((n1, 2 * n2 // FFT_GROUP, FFT_GROUP, ct), BF16),
                        pltpu.VMEM((FFT_GROUP, 2 * n1, ct), F32)],
        compiler_params=_cparams("parallel", "arbitrary"),
        name="hy_filter_spectrum",
    )(hf.reshape(h1, n2, D), hb.reshape(h1, n2, D), tab["gf"], tab["f"], abs_sum)


def _fft_conv(vv, x0, kf, skip, tab):
    B, ntile, L, ct = vv.shape
    D = ntile * ct
    n1, n2, h1 = tab["n1"], tab["n2"], tab["h1"]
    assert B % 2 == 0 and L == h1 * n2 and ct == FFT_LANES
    sa = n2 // FFT_GROUP
    sb = n1 // FFT_K1_GROUP
    sc = sa
    shape5 = (B // 2, 2, h1, n2, D)
    shape6 = (B // 2, 2, ntile, h1, n2, ct)
    a_idx = lambda s: jnp.where(s < sa, s, jnp.where(s >= sa + sb, s - sa - sb, sa - 1))
    c_idx = lambda s: jnp.clip(s - sa - sb, 0, sc - 1)
    blk = (None, 2, h1, FFT_GROUP, ct)
    blk6 = (None, 2, None, h1, FFT_GROUP, ct)
    in_specs = [
        pl.BlockSpec(blk6, lambda p, t, s: (p, 0, t, 0, a_idx(s), 0)),
        pl.BlockSpec(blk6, lambda p, t, s: (p, 0, t, 0, c_idx(s), 0)),
        pl.BlockSpec((FFT_GROUP, 2 * n1, 2 * h1), lambda p, t, s: (jnp.minimum(s, sa - 1), 0, 0)),
        pl.BlockSpec((2 * n2, 2 * n2), lambda p, t, s: (0, 0)),
        pl.BlockSpec((2 * n2, 2 * n2), lambda p, t, s: (0, 0)),
        pl.BlockSpec((FFT_K1_GROUP, 2 * n2, ct), lambda p, t, s: (jnp.clip(s - sa, 0, sb - 1), 0, t)),
        pl.BlockSpec((FFT_GROUP, 2 * h1, 2 * n1), lambda p, t, s: (c_idx(s), 0, 0)),
        pl.BlockSpec((1, ct), lambda p, t, s: (0, t)),
    ]
    kern = functools.partial(_fft_conv_kernel, sa=sa, sb=sb, n1=n1, n2=n2, h1=h1)
    y = pl.pallas_call(
        kern,
        out_shape=jax.ShapeDtypeStruct(shape5, BF16),
        grid=(B // 2, D // ct, sa + sb + sc),
        in_specs=in_specs,
        out_specs=pl.BlockSpec(blk, lambda p, t, s: (p, 0, 0, c_idx(s), t)),
        scratch_shapes=[pltpu.VMEM((n1, 2 * n2 // FFT_GROUP, FFT_GROUP, ct), BF16),
                        pltpu.VMEM((FFT_GROUP, 2 * n1, ct), F32),
                        pltpu.VMEM((FFT_GROUP, 2 * h1, ct), F32)],
        compiler_params=_cparams("parallel", "parallel", "arbitrary"),
        name="hy_fft_conv",
    )(vv.reshape(shape6), x0.reshape(shape6), tab["g"], tab["f"], tab["fi"], kf, tab["gi"],
      skip.reshape(1, D))
    return y.reshape(B, L, D)


def _filter_taps_kernel(feat_ref, t_ref, wf_ref, wb_ref, dl_ref, hf_ref, hb_ref, sum_ref, *, rows, total_rows):
    i = pl.program_id(1)
    window = jnp.exp(-t_ref[...] * dl_ref[...]) + HY_SHIFT
    feat = feat_ref[...]
    hf = jnp.dot(feat, wf_ref[...], preferred_element_type=F32, precision=lax.Precision.HIGHEST) * window
    hb = jnp.dot(feat, wb_ref[...], preferred_element_type=F32, precision=lax.Precision.HIGHEST) * window
    hf_ref[...] = hf
    hb_ref[...] = hb
    row = lax.broadcasted_iota(jnp.int32, hb.shape, 0) + i * rows
    part = (jnp.sum(jnp.abs(hf), axis=0, keepdims=True)
            + jnp.sum(jnp.where(row < total_rows - 1, jnp.abs(hb), 0.0), axis=0, keepdims=True))

    @pl.when(i == 0)
    def _():
        sum_ref[...] = part

    @pl.when(i > 0)
    def _():
        sum_ref[...] += part


def _filter_taps(L, w1, b1, w2, b2, w3, b3, w_out, freq):
    hp = lax.Precision.HIGHEST
    D = w_out.shape[-1] // 2
    t = jnp.linspace(0.0, 1.0, L, dtype=F32)[:, None]
    w = (2.0 * math.pi / L) * jnp.arange(L, dtype=F32)[:, None]
    bands = jnp.linspace(1e-4, HY_BANDS - 1, HY_BANDS, dtype=F32)[None, :]
    z = jnp.concatenate([t, jnp.cos(bands * w), -jnp.sin(bands * w)], axis=-1)
    h = jnp.sin(freq * (jnp.dot(z, w1, precision=hp) + b1))
    h = jnp.sin(freq * (jnp.dot(h, w2, precision=hp) + b2))
    h = jnp.sin(freq * (jnp.dot(h, w3, precision=hp) + b3))
    max_decay = math.log(HY_TARGET) / HY_FAST_DECAY
    min_decay = math.log(HY_TARGET) / HY_SLOW_DECAY
    deltas = jnp.abs(jnp.linspace(min_decay, max_decay, D, dtype=F32))[None, :]
    rows, ct = min(512, L), 512
    width = h.shape[-1]
    nct = D // ct
    tap_spec = pl.BlockSpec((rows, ct), lambda j, i: (i, j))
    return pl.pallas_call(
        functools.partial(_filter_taps_kernel, rows=rows, total_rows=L),
        out_shape=(jax.ShapeDtypeStruct((L, D), F32), jax.ShapeDtypeStruct((L, D), F32),
                   jax.ShapeDtypeStruct((1, D), F32)),
        grid=(nct, L // rows),
        in_specs=[pl.BlockSpec((rows, width), lambda j, i: (i, 0)),
                  pl.BlockSpec((rows, 1), lambda j, i: (i, 0)),
                  pl.BlockSpec((width, ct), lambda j, i: (0, j)),
                  pl.BlockSpec((width, ct), lambda j, i: (0, nct + j)),
                  pl.BlockSpec((1, ct), lambda j, i: (0, j))],
        out_specs=(tap_spec, tap_spec, pl.BlockSpec((1, ct), lambda j, i: (0, j))),
        compiler_params=_cparams("parallel", "arbitrary"),
        name="hy_filter_taps",
    )(h, t, w_out, w_out, deltas)


def _hyena_mixer(h, x, g1, p, tab):
    B, L, D = x.shape
    z = _matmul(h.reshape(B * L, D), p["w_in"], layer=p["slot"], bias=p["b_in"], out_dtype=BF16, name="hy_in",
                **HY_IN_TILES[p["slot"]]).reshape(B, L, 3 * D)
    x0, vv = _conv_gate(z, p["conv_w"], p["conv_b"])
    hf, hb, abs_sum = _filter_taps(L, p["f_w1"], p["f_b1"], p["f_w2"], p["f_b2"], p["f_w3"], p["f_b3"],
                                   p["f_wout"], p["freq"])
    y = _fft_conv(vv, x0, _filter_spectrum(hf, hb, abs_sum, tab), p["skip"], tab)
    out = _matmul(y.reshape(B * L, D), p["w_out"], layer=p["slot"], bias=p["b_out"],
                  resid=x.reshape(B * L, D), gate=g1, rows_per_gate=L, name="hy_out")
    return out.reshape(B, L, D)


STACKED_WEIGHTS = ("w_in", "w_out", "gate_w1")


def _layer_params(params, slot):
    p = {n: (v if n in STACKED_WEIGHTS else v[slot]) for n, v in params.items()}
    p["slot"] = slot
    return p


def _trunk(x, mod, hy, ret, gla, norm_g, mlp_w1, mlp_w2, final_g):
    B, L, D = x.shape
    tab = _dft_tables(L)
    tab["gf"] = _filter_tables(tab)
    for i in range(DEPTH):
        sh1, sc1, g1, sh2, sc2, g2 = jnp.split(mod[i], 6, axis=-1)
        h = _norm(x, norm_g[i, 0], sc1, sh1, out_dtype=BF16, name="norm_mix")
        kind, slot = i % N_MIXERS, i // N_MIXERS
        if kind == 0:
            x = _hyena_mixer(h, x, g1, _layer_params(hy, slot), tab)
        elif kind == 1:
            x = _retention_mixer(h, x, g1, _layer_params(ret, slot))
        else:
            x = _gla_mixer(h, x, g1, _layer_params(gla, slot))
        h = _norm(x, norm_g[i, 1], sc2, sh2, out_dtype=BF16, name="norm_mlp")
        a = _matmul(h.reshape(B * L, D), mlp_w1, layer=i, act="relu2", out_dtype=BF16, name="mlp_up",
                    **MLP_UP_TILES[i])
        x = _matmul_deep(a, mlp_w2, i, x.reshape(B * L, D), g2, L, name="mlp_down",
                         **MLP_DOWN_TILES[i]).reshape(B, L, D)
    return _norm(x, final_g, out_dtype=F32, name="norm_final")


def kernel(x_prompt, x_sample, c_prompt, c_sample, hy_w_in, hy_b_in, hy_conv_w, hy_conv_b, hy_f_w1, hy_f_b1, hy_f_w2, hy_f_b2, hy_f_w3, hy_f_b3, hy_f_wout, hy_freq, hy_skip, hy_w_out, hy_b_out, ret_w_in, ret_w_out, gla_w_in, gla_gate_w1, gla_gate_w2, gla_gate_b, gla_w_out, norm_g, ada_w, ada_b, mlp_w1, mlp_w2, final_g):
    D = x_prompt.shape[-1]
    bf = lambda w: w.astype(BF16)
    hy = dict(w_in=bf(hy_w_in), b_in=hy_b_in, conv_w=hy_conv_w, conv_b=hy_conv_b,
              f_w1=hy_f_w1, f_b1=hy_f_b1, f_w2=hy_f_w2, f_b2=hy_f_b2, f_w3=hy_f_w3, f_b3=hy_f_b3,
              f_wout=hy_f_wout, freq=hy_freq, skip=hy_skip, w_out=bf(hy_w_out), b_out=hy_b_out)

    perm = _pair_split_perm(D, RET_HEADS)
    cols = np.concatenate([perm, D + perm, np.arange(2 * D, 4 * D)])
    ret = dict(w_in=bf(ret_w_in[:, :, cols]), w_out=bf(ret_w_out))

    n_gla, _, _, rank = gla_gate_w1.shape
    w1 = jnp.concatenate([gla_gate_w1[:, 0], gla_gate_w1[:, 1],
                          jnp.zeros((n_gla, D, V7X_LANES - 2 * rank), F32)], axis=-1)
    w2 = jnp.zeros((n_gla, 2, V7X_LANES, gla_gate_w2.shape[-1]), F32)
    w2 = w2.at[:, 0, :rank].set(gla_gate_w2[:, 0]).at[:, 1, rank:2 * rank].set(gla_gate_w2[:, 1])
    gla = dict(w_in=bf(gla_w_in), gate_w1=bf(w1), gate_w2=w2, gate_b=gla_gate_b[:, :, None, :],
               w_out=bf(gla_w_out))
    mw1, mw2 = bf(mlp_w1), bf(mlp_w2)

    nb_p, nb_s = c_prompt.shape[0], c_sample.shape[0]
    cs = jax.nn.silu(jnp.concatenate([c_prompt, c_sample], axis=0))
    rows = -(-cs.shape[0] // 16) * 16
    cs = jnp.pad(cs, ((0, rows - cs.shape[0]), (0, 0))).astype(BF16)
    mod = jnp.stack([_matmul(cs, ada_w, layer=i, bias=ada_b[i], bn=512, name="ada") for i in range(DEPTH)])
    mod_p, mod_s = mod[:, :nb_p], mod[:, nb_p:nb_p + nb_s]

    y_prompt = _trunk(x_prompt, mod_p, hy, ret, gla, norm_g, mw1, mw2, final_g)
    y_sample = _trunk(x_sample, mod_s, hy, ret, gla, norm_g, mw1, mw2, final_g)
    return (y_prompt, y_sample)
```

```python
import functools
import math

import jax
import jax.numpy as jnp
import numpy as np
from jax import lax
from jax.experimental import pallas as pl
from jax.experimental.pallas import tpu as pltpu

D_MODEL = 4096
DEPTH = 4
N_MIXERS = 3
EPS = 1e-6
HY_BANDS = 16
HY_FAST_DECAY = 0.3
HY_SLOW_DECAY = 1.5
HY_TARGET = 1e-2
HY_SHIFT = 0.05
RET_HEADS = 16
ROPE_BASE = 10000.0
GLA_HEADS = 4
GLA_GATE_RANK = 16
GLA_TAU = 16.0

F32 = jnp.float32
BF16 = jnp.bfloat16

V7X_VMEM_LIMIT_BYTES = 56 * 1024 * 1024
V7X_LANES = 128
MM_BLOCK_M = 1024
MM_BLOCK_N = 1024
MM_BLOCK_K = 4096
MM_DEEP_BLOCK_N = 512
MM_DEEP_BLOCK_K = 4096
NORM_BLOCK_ROWS = 512
RET_CHUNK = 256
RET_HEADS_PER_STEP = 4
GLA_CHUNK = 128
GLA_SUBBLOCK = 32
GLA_EXP_CLAMP = 80.0


def _cparams(*sem):
    return pltpu.CompilerParams(dimension_semantics=sem, vmem_limit_bytes=V7X_VMEM_LIMIT_BYTES)


def _mm_kernel(*refs, nk, has_bias, has_resid, act):
    it = iter(refs)
    x_ref, w_ref = next(it), next(it)
    b_ref = next(it) if has_bias else None
    r_ref = next(it) if has_resid else None
    g_ref = next(it) if has_resid else None
    o_ref = next(it)
    acc_ref = next(it) if nk > 1 else None

    part = jnp.dot(x_ref[...].astype(BF16), w_ref[...].astype(BF16), preferred_element_type=F32)

    def finish(acc):
        if has_bias:
            acc = acc + b_ref[...]
        if act == "relu2":
            acc = jnp.maximum(acc, 0.0)
            acc = acc * acc
        if has_resid:
            acc = r_ref[...] + g_ref[...] * acc
        o_ref[...] = acc.astype(o_ref.dtype)

    if nk == 1:
        finish(part)
    else:
        k = pl.program_id(2)

        @pl.when(k == 0)
        def _():
            acc_ref[...] = part

        @pl.when(jnp.logical_and(k > 0, k < nk - 1))
        def _():
            acc_ref[...] += part

        @pl.when(k == nk - 1)
        def _():
            finish(acc_ref[...] + part)


def _matmul(x, w, *, layer=None, bias=None, act=None, resid=None, gate=None, rows_per_gate=None,
            out_dtype=F32, bm=MM_BLOCK_M, bn=MM_BLOCK_N, bk=MM_BLOCK_K, name="matmul"):
    M, K = x.shape
    K2, N = w.shape[-2:]
    assert K == K2
    bm, bn, bk = min(bm, M, rows_per_gate or M), min(bn, N), min(bk, K)
    assert M % bm == 0 and N % bn == 0 and K % bk == 0
    nk = K // bk
    has_bias = bias is not None
    has_resid = resid is not None

    operands = [x, w]
    in_specs = [pl.BlockSpec((bm, bk), lambda i, j, k: (i, k)),
                pl.BlockSpec((bk, bn), lambda i, j, k: (k, j)) if layer is None else
                pl.BlockSpec((None, bk, bn), lambda i, j, k: (layer, k, j))]
    if has_bias:
        operands.append(bias.reshape(1, N).astype(F32))
        in_specs.append(pl.BlockSpec((1, bn), lambda i, j, k: (0, j)))
    if has_resid:
        assert rows_per_gate % bm == 0
        tiles_per_gate = rows_per_gate // bm
        operands.append(resid)
        in_specs.append(pl.BlockSpec((bm, bn), lambda i, j, k: (i, j)))
        operands.append(gate.reshape(gate.shape[0], 1, N).astype(F32))
        in_specs.append(pl.BlockSpec((None, 1, bn), lambda i, j, k: (i // tiles_per_gate, 0, j)))

    scratch = [pltpu.VMEM((bm, bn), F32)] if nk > 1 else []
    kern = functools.partial(_mm_kernel, nk=nk, has_bias=has_bias, has_resid=has_resid, act=act)
    return pl.pallas_call(
        kern,
        out_shape=jax.ShapeDtypeStruct((M, N), out_dtype),
        grid=(M // bm, N // bn, nk),
        in_specs=in_specs,
        out_specs=pl.BlockSpec((bm, bn), lambda i, j, k: (i, j)),
        scratch_shapes=scratch,
        compiler_params=_cparams("parallel", "parallel", "arbitrary"),
        name=name,
    )(*operands)


def _mm_deep_kernel(x_ref, w_ref, r_ref, g_ref, o_ref, acc_ref, *, nk, bn):
    k = pl.program_id(1)
    cols = pl.ds(pl.multiple_of(pl.program_id(2) * bn, bn), bn)
    part = jnp.dot(x_ref[...], w_ref[...], preferred_element_type=F32)

    @pl.when(k == 0)
    def _():
        acc_ref[:, cols] = part

    @pl.when(jnp.logical_and(k > 0, k < nk - 1))
    def _():
        acc_ref[:, cols] += part

    @pl.when(k == nk - 1)
    def _():
        o_ref[...] = r_ref[...] + g_ref[...] * (acc_ref[:, cols] + part)


def _matmul_deep(x, w, layer, resid, gate, rows_per_gate, *, bm=MM_BLOCK_M, bn=MM_DEEP_BLOCK_N,
                 bk=MM_DEEP_BLOCK_K, name):
    M, K = x.shape
    N = w.shape[-1]
    bm = min(bm, M, rows_per_gate)
    assert M % bm == 0 and N % bn == 0 and K % bk == 0 and rows_per_gate % bm == 0
    nk = K // bk
    assert nk >= 2
    tiles_per_gate = rows_per_gate // bm
    jcol = lambda k, j: jnp.where(k == nk - 1, j, 0)
    return pl.pallas_call(
        functools.partial(_mm_deep_kernel, nk=nk, bn=bn),
        out_shape=jax.ShapeDtypeStruct((M, N), F32),
        grid=(M // bm, nk, N // bn),
        in_specs=[pl.BlockSpec((bm, bk), lambda i, k, j: (i, k)),
                  pl.BlockSpec((None, bk, bn), lambda i, k, j: (layer, k, j)),
                  pl.BlockSpec((bm, bn), lambda i, k, j: (i, jcol(k, j))),
                  pl.BlockSpec((None, 1, bn), lambda i, k, j: (i // tiles_per_gate, 0, jcol(k, j)))],
        out_specs=pl.BlockSpec((bm, bn), lambda i, k, j: (i, jcol(k, j))),
        scratch_shapes=[pltpu.VMEM((bm, N), F32)],
        compiler_params=_cparams("parallel", "arbitrary", "arbitrary"),
        name=name,
    )(x, w, resid, gate.reshape(gate.shape[0], 1, N).astype(F32))


def _norm_kernel(x_ref, g_ref, *rest, modulate):
    if modulate:
        sc_ref, sh_ref, o_ref = rest
    else:
        (o_ref,) = rest
    x = x_ref[...]
    y = x * lax.rsqrt(jnp.mean(x * x, axis=-1, keepdims=True) + EPS) * g_ref[...]
    if modulate:
        y = y * (1.0 + sc_ref[...]) + sh_ref[...]
    o_ref[...] = y.astype(o_ref.dtype)


def _norm(x, g, scale=None, shift=None, *, out_dtype, name):
    B, L, D = x.shape
    rows = min(NORM_BLOCK_ROWS, L)
    assert L % rows == 0
    modulate = scale is not None
    operands = [x, g.reshape(1, D)]
    in_specs = [pl.BlockSpec((None, rows, D), lambda b, i: (b, i, 0)),
                pl.BlockSpec((1, D), lambda b, i: (0, 0))]
    if modulate:
        operands += [scale.reshape(B, 1, D), shift.reshape(B, 1, D)]
        in_specs += [pl.BlockSpec((None, 1, D), lambda b, i: (b, 0, 0))] * 2
    return pl.pallas_call(
        functools.partial(_norm_kernel, modulate=modulate),
        out_shape=jax.ShapeDtypeStruct((B, L, D), out_dtype),
        grid=(B, L // rows),
        in_specs=in_specs,
        out_specs=pl.BlockSpec((None, rows, D), lambda b, i: (b, i, 0)),
        compiler_params=_cparams("parallel", "parallel"),
        name=name,
    )(*operands)


def _silu(x):
    return x / (1.0 + jnp.exp(-x))


def _gated_head_norm(o, gate):
    o = o * lax.rsqrt(jnp.mean(o * o, axis=-1, keepdims=True) + EPS)
    return _silu(gate) * o


def _ret_kernel(lg_ref, q_ref, k_ref, v_ref, cos_ref, sin_ref, *rest, reverse, final, heads_per_step, hd):
    if final:
        op_ref, g_ref, o_ref, state_ref = rest
    else:
        o_ref, state_ref = rest
    C = q_ref.shape[0]
    half = hd // 2

    @pl.when(pl.program_id(2) == 0)
    def _():
        state_ref[...] = jnp.zeros_like(state_ref)

    cos, sin = cos_ref[...], sin_ref[...]
    row = lax.broadcasted_iota(jnp.int32, (C, C), 0)
    col = lax.broadcasted_iota(jnp.int32, (C, C), 1)
    ridx = lax.broadcasted_iota(jnp.int32, (C, 1), 0).astype(F32)
    if reverse:
        mask = col > row
        dist = (col - row).astype(F32)
        q_pow, k_pow = C - ridx, ridx
    else:
        mask = col <= row
        dist = (row - col).astype(F32)
        q_pow, k_pow = ridx + 1.0, C - 1.0 - ridx
    dist = jnp.where(mask, dist, 0.0)

    def rotate(x):
        x1, x2 = x[:, :half], x[:, half:]
        return jnp.concatenate([x1 * cos - x2 * sin, x1 * sin + x2 * cos], axis=1)

    for hh in range(heads_per_step):
        lg = lg_ref[pl.program_id(1) * heads_per_step + hh]
        sl = slice(hh * hd, (hh + 1) * hd)
        q = rotate(q_ref[:, sl].astype(F32))
        k = rotate(k_ref[:, sl].astype(F32)) * (hd ** -0.5)
        v = v_ref[:, sl].astype(BF16)
        intra = jnp.where(mask, jnp.exp(lg * dist), 0.0)
        s = lax.dot_general(q.astype(BF16), k.astype(BF16), (((1,), (1,)), ((), ())),
                            preferred_element_type=F32) * intra
        state = state_ref[hh]
        o = (jnp.dot(s.astype(BF16), v, preferred_element_type=F32)
             + jnp.dot((q * jnp.exp(lg * q_pow)).astype(BF16), state.astype(BF16),
                       preferred_element_type=F32))
        kd = (k * jnp.exp(lg * k_pow)).astype(BF16)
        state_ref[hh] = state * jnp.exp(lg * jnp.full((1, hd), C, F32)) + lax.dot_general(
            kd, v, (((0,), (0,)), ((), ())), preferred_element_type=F32)
        if final:
            o_ref[:, sl] = _gated_head_norm(op_ref[:, sl].astype(F32) + o,
                                            g_ref[:, sl].astype(F32)).astype(o_ref.dtype)
        else:
            o_ref[:, sl] = o.astype(o_ref.dtype)


def _ret_pass(z, log_gamma, cos, sin, o_prev, *, reverse):
    B, L, D4 = z.shape
    D = D4 // 4
    hd = D // RET_HEADS
    hps = RET_HEADS_PER_STEP
    W = hps * hd
    C = min(RET_CHUNK, L)
    n = L // C
    nblk = D // W
    final = o_prev is not None
    cidx = (lambda c: n - 1 - c) if reverse else (lambda c: c)

    def zspec(part):
        return pl.BlockSpec((None, C, W), lambda b, h, c, lg: (b, cidx(c), part * nblk + h))

    tspec = pl.BlockSpec((C, hd // 2), lambda b, h, c, lg: (cidx(c), 0))
    ospec = pl.BlockSpec((None, C, W), lambda b, h, c, lg: (b, cidx(c), h))
    operands = [z, z, z, cos, sin]
    in_specs = [zspec(0), zspec(1), zspec(2), tspec, tspec]
    if final:
        operands += [o_prev, z]
        in_specs += [ospec, zspec(3)]
    kern = functools.partial(_ret_kernel, reverse=reverse, final=final, heads_per_step=hps, hd=hd)
    return pl.pallas_call(
        kern,
        out_shape=jax.ShapeDtypeStruct((B, L, D), BF16),
        grid_spec=pltpu.PrefetchScalarGridSpec(
            num_scalar_prefetch=1,
            grid=(B, nblk, n),
            in_specs=in_specs,
            out_specs=ospec,
            scratch_shapes=[pltpu.VMEM((hps, hd, hd), F32)]),
        compiler_params=_cparams("parallel", "parallel", "arbitrary"),
        name="ret_bwd" if reverse else "ret_fwd",
    )(log_gamma, *operands)


def _pair_split_perm(d, heads):
    hd = d // heads
    idx = np.arange(d).reshape(heads, hd)
    return np.concatenate([idx[:, 0::2], idx[:, 1::2]], axis=1).reshape(-1)


def _retention_mixer(h, x, g1, p):
    B, L, D = x.shape
    hd = D // RET_HEADS
    z = _matmul(h.reshape(B * L, D), p["w_in"], layer=p["slot"], out_dtype=BF16,
                name="ret_in").reshape(B, L, 4 * D)
    pos = jnp.arange(L, dtype=F32)
    inv = 1.0 / (ROPE_BASE ** jnp.linspace(0.0, 1.0, hd // 2, dtype=F32))
    ang = pos[:, None] * inv[None, :]
    cos, sin = jnp.cos(ang), jnp.sin(ang)
    hidx = jnp.arange(RET_HEADS, dtype=F32)
    lg_fwd = jnp.log1p(-jnp.exp2(-5.0 - hidx))
    lg_bwd = jnp.log1p(-jnp.exp2(-5.0 - hidx[::-1]))
    o_fwd = _ret_pass(z, lg_fwd, cos, sin, None, reverse=False)
    y = _ret_pass(z, lg_bwd, cos, sin, o_fwd, reverse=True)
    out = _matmul(y.reshape(B * L, D), p["w_out"], layer=p["slot"], resid=x.reshape(B * L, D), gate=g1,
                  rows_per_gate=L, name="ret_out")
    return out.reshape(B, L, D)


def _gla_kernel(q_ref, k_ref, v_ref, t_ref, w2_ref, gb_ref, *rest, reverse, final, sub):
    if final:
        op_ref, r_ref, o_ref, state_ref, a_ref = rest
    else:
        o_ref, state_ref, a_ref = rest
    C, dk = q_ref.shape

    @pl.when(pl.program_id(2) == 0)
    def _():
        state_ref[...] = jnp.zeros_like(state_ref)

    zg = jnp.dot(t_ref[...].astype(BF16), w2_ref[...].astype(BF16), preferred_element_type=F32) + gb_ref[...]
    log_a = (jnp.minimum(zg, 0.0) - jnp.log1p(jnp.exp(-jnp.abs(zg)))) * (1.0 / GLA_TAU)

    row = lax.broadcasted_iota(jnp.int32, (C, C), 0)
    col = lax.broadcasted_iota(jnp.int32, (C, C), 1)
    tri = ((col >= row) if reverse else (col <= row)).astype(BF16)
    hi = log_a.astype(BF16)
    lo = (log_a - hi.astype(F32)).astype(BF16)
    b = (jnp.dot(tri, hi, preferred_element_type=F32) + jnp.dot(tri, lo, preferred_element_type=F32))
    b_tot = jnp.sum(log_a, axis=0, keepdims=True)

    q = q_ref[...].astype(F32) * (dk ** -0.5)
    k = k_ref[...].astype(F32)
    v = v_ref[...].astype(BF16)

    for blk in range(C // sub):
        rows = slice(blk * sub, (blk + 1) * sub)
        ref = b[blk * sub + sub // 2:blk * sub + sub // 2 + 1, :]
        qh = q[rows] * jnp.exp(jnp.minimum(b[rows] - ref, GLA_EXP_CLAMP))
        kh = k * jnp.exp(jnp.minimum(ref - b, GLA_EXP_CLAMP))
        s = lax.dot_general(qh.astype(BF16), kh.astype(BF16), (((1,), (1,)), ((), ())),
                            preferred_element_type=F32)
        r_i = lax.broadcasted_iota(jnp.int32, (sub, C), 0) + blk * sub
        c_i = lax.broadcasted_iota(jnp.int32, (sub, C), 1)
        keep = (c_i > r_i) if reverse else (c_i <= r_i)
        a_ref[rows, :] = jnp.where(keep, s, 0.0).astype(BF16)

    state = state_ref[...]
    o = (jnp.dot(a_ref[...], v, preferred_element_type=F32)
         + lax.dot_general((q * jnp.exp(b)).astype(BF16), state.astype(BF16), (((1,), (1,)), ((), ())),
                           preferred_element_type=F32))
    kd = (k * jnp.exp(b_tot - b)).astype(BF16)
    state_ref[...] = state * jnp.exp(b_tot) + lax.dot_general(
        v, kd, (((0,), (0,)), ((), ())), preferred_element_type=F32)
    if final:
        o_ref[...] = _gated_head_norm(op_ref[...].astype(F32) + o, r_ref[...].astype(F32)).astype(o_ref.dtype)
    else:
        o_ref[...] = o.astype(o_ref.dtype)


def _gla_pass(z, t, w2, gb, o_prev, *, direction):
    B, L, D3 = z.shape
    D = D3 // 3
    dk = D // 2 // GLA_HEADS
    dv = D // GLA_HEADS
    C = min(GLA_CHUNK, L)
    n = L // C
    reverse = direction == 1
    final = o_prev is not None
    cidx = (lambda c: n - 1 - c) if reverse else (lambda c: c)
    H = GLA_HEADS

    qspec = pl.BlockSpec((None, C, dk), lambda b, h, c: (b, cidx(c), h))
    kspec = pl.BlockSpec((None, C, dk), lambda b, h, c: (b, cidx(c), H + h))
    vspec = pl.BlockSpec((None, C, dv), lambda b, h, c: (b, cidx(c), H + h))
    rspec = pl.BlockSpec((None, C, dv), lambda b, h, c: (b, cidx(c), 2 * H + h))
    ospec = pl.BlockSpec((None, C, dv), lambda b, h, c: (b, cidx(c), h))
    tspec = pl.BlockSpec((None, C, t.shape[-1]), lambda b, h, c: (b, cidx(c), 0))
    w2spec = pl.BlockSpec((None, t.shape[-1], dk), lambda b, h, c: (direction, 0, h))
    gbspec = pl.BlockSpec((None, 1, dk), lambda b, h, c: (direction, 0, h))
    operands = [z, z, z, t, w2, gb]
    in_specs = [qspec, kspec, vspec, tspec, w2spec, gbspec]
    if final:
        operands += [o_prev, z]
        in_specs += [ospec, rspec]
    kern = functools.partial(_gla_kernel, reverse=reverse, final=final, sub=min(GLA_SUBBLOCK, C))
    return pl.pallas_call(
        kern,
        out_shape=jax.ShapeDtypeStruct((B, L, D), BF16),
        grid=(B, H, n),
        in_specs=in_specs,
        out_specs=ospec,
        scratch_shapes=[pltpu.VMEM((dv, dk), F32), pltpu.VMEM((C, C), BF16)],
        compiler_params=_cparams("parallel", "parallel", "arbitrary"),
        name="gla_bwd" if reverse else "gla_fwd",
    )(*operands)


def _gla_mixer(h, x, g1, p):
    B, L, D = x.shape
    h2 = h.reshape(B * L, D)
    z = _matmul(h2, p["w_in"], layer=p["slot"], out_dtype=BF16, name="gla_in").reshape(B, L, 3 * D)
    t = _matmul(h2, p["gate_w1"], layer=p["slot"], bn=V7X_LANES, name="gla_gate").reshape(B, L, V7X_LANES)
    o_fwd = _gla_pass(z, t, p["gate_w2"], p["gate_b"], None, direction=0)
    y = _gla_pass(z, t, p["gate_w2"], p["gate_b"], o_fwd, direction=1)
    out = _matmul(y.reshape(B * L, D), p["w_out"], layer=p["slot"], resid=x.reshape(B * L, D), gate=g1,
                  rows_per_gate=L, name="gla_out")
    return out.reshape(B, L, D)


CONV_HALO = 16


def _conv_gate_kernel(*refs, rows):
    groups = [refs[5 * g:5 * g + 5] for g in range(3)]
    x0_out, vv_out = refs[15:]
    i = pl.program_id(1)
    first = i == 0
    last = i == pl.num_programs(1) - 1
    h = CONV_HALO
    ct = groups[0][0].shape[1]
    row = lax.broadcasted_iota(jnp.int32, (rows, ct), 0)

    def put(out_ref, val):
        lanes = out_ref.shape[-1]
        for t in range(out_ref.shape[0]):
            out_ref[t] = val[:, t * lanes:(t + 1) * lanes].astype(out_ref.dtype)

    def conv(cur_ref, prev_ref, next_ref, w_ref, b_ref):
        cur = cur_ref[...].astype(F32)
        before = jnp.where(first, 0.0, prev_ref[h - 1:h, :].astype(F32))
        after = jnp.where(last, 0.0, next_ref[0:1, :].astype(F32))
        prev = jnp.where(row == 0, before, pltpu.roll(cur, 1, axis=0))
        nxt = jnp.where(row == rows - 1, after, pltpu.roll(cur, rows - 1, axis=0))
        w = w_ref[...]
        return prev * w[0:1] + cur * w[1:2] + nxt * w[2:3] + b_ref[...]

    put(x0_out, conv(*groups[0]))
    x1 = conv(*groups[1])
    put(vv_out, conv(*groups[2]) * x1)


def _conv_gate(z, conv_w, conv_b):
    B, L, D3 = z.shape
    D = D3 // 3
    rows, ct = min(512, L), 512
    nct = D // ct
    h = CONV_HALO
    nhalo = L // h
    operands, in_specs = [], []
    for g in range(3):
        operands += [z, z, z, conv_w, conv_b.reshape(1, D3)]
        in_specs += [
            pl.BlockSpec((None, rows, ct), lambda b, i, j, g=g: (b, i, g * nct + j)),
            pl.BlockSpec((None, h, ct), lambda b, i, j, g=g: (b, jnp.maximum(i * (rows // h) - 1, 0), g * nct + j)),
            pl.BlockSpec((None, h, ct), lambda b, i, j, g=g: (b, jnp.minimum((i + 1) * (rows // h), nhalo - 1), g * nct + j)),
            pl.BlockSpec((3, ct), lambda b, i, j, g=g: (0, g * nct + j)),
            pl.BlockSpec((1, ct), lambda b, i, j, g=g: (0, g * nct + j)),
        ]
    tiles = ct // FFT_LANES
    ospec = pl.BlockSpec((None, tiles, rows, FFT_LANES), lambda b, i, j: (b, j, i, 0))
    oshape = jax.ShapeDtypeStruct((B, D // FFT_LANES, L, FFT_LANES), BF16)
    return pl.pallas_call(
        functools.partial(_conv_gate_kernel, rows=rows),
        out_shape=(oshape, oshape),
        grid=(B, L // rows, nct),
        in_specs=in_specs,
        out_specs=(ospec, ospec),
        compiler_params=_cparams("parallel", "parallel", "parallel"),
        name="hy_conv_gate",
    )(*operands)


FFT_GROUP = 16
FFT_K1_GROUP = 8
FFT_LANES = 256


def _stage_a(x_parts, g_ref, z_ref, stage_ref, step, *, n1, ng, mirror_second=False):
    xt = [jnp.swapaxes(x, 0, 1) for x in x_parts]
    for j in range(FFT_GROUP):
        x = jnp.concatenate([xt[0][j], xt[1][FFT_GROUP - 1 - j if mirror_second else j]], axis=0)
        stage_ref[j] = jnp.dot(g_ref[j], x.astype(BF16), preferred_element_type=F32)
    a = jnp.swapaxes(stage_ref[...], 0, 1)
    z_ref[:, step] = a[:n1].astype(z_ref.dtype)
    z_ref[:, ng + step] = a[n1:].astype(z_ref.dtype)


def _fft_conv_kernel(x_ref, x0_ref, g_ref, f_ref, fi_ref, kf_ref, gi_ref, skip_ref, o_ref, z_ref, sa_ref, sc_ref,
                     *, sa, sb, n1, n2, h1):
    step = pl.program_id(2)
    ng = n2 // FFT_GROUP

    @pl.when(step < sa)
    def _():
        _stage_a([x_ref[0].astype(F32), x_ref[1].astype(F32)], g_ref, z_ref, sa_ref, step, n1=n1, ng=ng)

    @pl.when(jnp.logical_and(step >= sa, step < sa + sb))
    def _():
        for j in range(FFT_K1_GROUP):
            k1 = (step - sa) * FFT_K1_GROUP + j
            zk = z_ref[k1].reshape(2 * n2, z_ref.shape[-1])
            y = jnp.dot(f_ref[...], zk, preferred_element_type=F32)
            yr, yi = y[:n2], y[n2:]
            kr, ki = kf_ref[j, :n2, :], kf_ref[j, n2:, :]
            p = jnp.concatenate([yr * kr - yi * ki, yr * ki + yi * kr], axis=0).astype(BF16)
            back = jnp.dot(fi_ref[...], p, preferred_element_type=F32)
            z_ref[k1] = back.astype(z_ref.dtype).reshape(z_ref.shape[1:])

    @pl.when(step >= sa + sb)
    def _():
        g = step - sa - sb
        zr = jnp.swapaxes(z_ref[:, g].astype(F32), 0, 1)
        zi = jnp.swapaxes(z_ref[:, ng + g].astype(F32), 0, 1)
        for j in range(FFT_GROUP):
            bz = jnp.concatenate([zr[j], zi[j]], axis=0).astype(BF16)
            sc_ref[j] = jnp.dot(gi_ref[j], bz, preferred_element_type=F32)
        y = jnp.swapaxes(sc_ref[...], 0, 1).reshape(x_ref.shape)
        o_ref[...] = ((y + skip_ref[...] * x_ref[...].astype(F32)) * x0_ref[...].astype(F32)).astype(o_ref.dtype)


def _dft_tables(L):
    N = 2 * L
    n1 = n2 = int(round(math.sqrt(N)))
    assert n1 * n2 == N and n2 % FFT_GROUP == 0 and n1 % FFT_K1_GROUP == 0
    h1 = n1 // 2
    k1 = jnp.arange(n1, dtype=jnp.int32)
    m = (jnp.arange(n2, dtype=jnp.int32)[:, None, None] * k1[None, :, None]
         + n2 * k1[None, :, None] * jnp.arange(h1, dtype=jnp.int32)[None, None, :]) % N
    ang = (2.0 * math.pi / N) * m.astype(F32)
    c, s = jnp.cos(ang), jnp.sin(ang)
    block = lambda re, im: jnp.concatenate([jnp.concatenate([re, -im], -1), jnp.concatenate([im, re], -1)], -2)
    g = block(c, -s).astype(BF16)
    ct_, st_ = jnp.swapaxes(c, 1, 2) / N, jnp.swapaxes(s, 1, 2) / N
    gi = block(ct_, st_).astype(BF16)
    idx = jnp.arange(n2, dtype=jnp.int32)
    ang2 = (2.0 * math.pi / n2) * ((idx[:, None] * idx[None, :]) % n2).astype(F32)
    c2, s2 = jnp.cos(ang2), jnp.sin(ang2)
    return dict(n1=n1, n2=n2, h1=h1, g=g, gi=gi, f=block(c2, -s2).astype(BF16), fi=block(c2, s2).astype(BF16))


def _filter_spectrum_kernel(hf_ref, hb_ref, g_ref, f_ref, sum_ref, o_ref, z_ref, sa_ref, *, sa, n1, n2):
    step = pl.program_id(1)
    ng = n2 // FFT_GROUP

    @pl.when(step < sa)
    def _():
        _stage_a([hf_ref[...], hb_ref[...]], g_ref, z_ref, sa_ref, step, n1=n1, ng=ng, mirror_second=True)

    @pl.when(step >= sa)
    def _():
        scale = 1.0 / (sum_ref[...] + EPS)
        for j in range(FFT_K1_GROUP):
            k1 = (step - sa) * FFT_K1_GROUP + j
            zk = z_ref[k1].reshape(2 * n2, z_ref.shape[-1])
            o_ref[j] = (jnp.dot(f_ref[...], zk, preferred_element_type=F32) * scale).astype(o_ref.dtype)


def _filter_tables(tab):
    n1, n2, h1 = tab["n1"], tab["n2"], tab["h1"]
    N = n1 * n2
    k1 = jnp.arange(n1, dtype=jnp.int32)[None, :, None]
    i2 = jnp.arange(n2, dtype=jnp.int32)[:, None, None]
    a = jnp.arange(h1, dtype=jnp.int32)[None, None, :]
    rows = jnp.concatenate([a, n1 - 1 - a], axis=-1)
    m = (i2 * k1 + n2 * k1 * rows) % N
    ang = (2.0 * math.pi / N) * m.astype(F32)
    col = jnp.arange(2 * h1, dtype=jnp.int32)[None, None, :]
    live = jnp.logical_not(jnp.logical_and(i2 == 0, col == 2 * h1 - 1))
    re = jnp.where(live, jnp.cos(ang), 0.0)
    im = jnp.where(live, -jnp.sin(ang), 0.0)
    return jnp.concatenate([re, im], axis=1).astype(BF16)


def _filter_spectrum(hf, hb, abs_sum, tab):
    L, D = hf.shape
    n1, n2, h1 = tab["n1"], tab["n2"], tab["h1"]
    ct = FFT_LANES
    sa = n2 // FFT_GROUP
    sb = n1 // FFT_K1_GROUP
    kern = functools.partial(_filter_spectrum_kernel, sa=sa, n1=n1, n2=n2)
    return pl.pallas_call(
        kern,
        out_shape=jax.ShapeDtypeStruct((n1, 2 * n2, D), BF16),
        grid=(D // ct, sa + sb),
        in_specs=[
            pl.BlockSpec((h1, FFT_GROUP, ct), lambda t, s: (0, jnp.minimum(s, sa - 1), t)),
            pl.BlockSpec((h1, FFT_GROUP, ct), lambda t, s: (0, sa - 1 - jnp.minimum(s, sa - 1), t)),
            pl.BlockSpec((FFT_GROUP, 2 * n1, 2 * h1), lambda t, s: (jnp.minimum(s, sa - 1), 0, 0)),
            pl.BlockSpec((2 * n2, 2 * n2), lambda t, s: (0, 0)),
            pl.BlockSpec((1, ct), lambda t, s: (0, t)),
        ],
        out_specs=pl.BlockSpec((FFT_K1_GROUP, 2 * n2, ct), lambda t, s: (jnp.maximum(s - sa, 0), 0, t)),
        scratch_shapes=[pltpu.VMEM((n1, 2 * n2 // FFT_GROUP, FFT_GROUP, ct), BF16),
                        pltpu.VMEM((FFT_GROUP, 2 * n1, ct), F32)],
        compiler_params=_cparams("parallel", "arbitrary"),
        name="hy_filter_spectrum",
    )(hf.reshape(h1, n2, D), hb.reshape(h1, n2, D), tab["gf"], tab["f"], abs_sum)


def _fft_conv(vv, x0, kf, skip, tab):
    B, ntile, L, ct = vv.shape
    D = ntile * ct
    n1, n2, h1 = tab["n1"], tab["n2"], tab["h1"]
    assert B % 2 == 0 and L == h1 * n2 and ct == FFT_LANES
    sa = n2 // FFT_GROUP
    sb = n1 // FFT_K1_GROUP
    sc = sa
    shape5 = (B // 2, 2, h1, n2, D)
    shape6 = (B // 2, 2, ntile, h1, n2, ct)
    a_idx = lambda s: jnp.where(s < sa, s, jnp.where(s >= sa + sb, s - sa - sb, sa - 1))
    c_idx = lambda s: jnp.clip(s - sa - sb, 0, sc - 1)
    blk = (None, 2, h1, FFT_GROUP, ct)
    blk6 = (None, 2, None, h1, FFT_GROUP, ct)
    in_specs = [
        pl.BlockSpec(blk6, lambda p, t, s: (p, 0, t, 0, a_idx(s), 0)),
        pl.BlockSpec(blk6, lambda p, t, s: (p, 0, t, 0, c_idx(s), 0)),
        pl.BlockSpec((FFT_GROUP, 2 * n1, 2 * h1), lambda p, t, s: (jnp.minimum(s, sa - 1), 0, 0)),
        pl.BlockSpec((2 * n2, 2 * n2), lambda p, t, s: (0, 0)),
        pl.BlockSpec((2 * n2, 2 * n2), lambda p, t, s: (0, 0)),
        pl.BlockSpec((FFT_K1_GROUP, 2 * n2, ct), lambda p, t, s: (jnp.clip(s - sa, 0, sb - 1), 0, t)),
        pl.BlockSpec((FFT_GROUP, 2 * h1, 2 * n1), lambda p, t, s: (c_idx(s), 0, 0)),
        pl.BlockSpec((1, ct), lambda p, t, s: (0, t)),
    ]
    kern = functools.partial(_fft_conv_kernel, sa=sa, sb=sb, n1=n1, n2=n2, h1=h1)
    y = pl.pallas_call(
        kern,
        out_shape=jax.ShapeDtypeStruct(shape5, BF16),
        grid=(B // 2, D // ct, sa + sb + sc),
        in_specs=in_specs,
        out_specs=pl.BlockSpec(blk, lambda p, t, s: (p, 0, 0, c_idx(s), t)),
        scratch_shapes=[pltpu.VMEM((n1, 2 * n2 // FFT_GROUP, FFT_GROUP, ct), BF16),
                        pltpu.VMEM((FFT_GROUP, 2 * n1, ct), F32),
                        pltpu.VMEM((FFT_GROUP, 2 * h1, ct), F32)],
        compiler_params=_cparams("parallel", "parallel", "arbitrary"),
        name="hy_fft_conv",
    )(vv.reshape(shape6), x0.reshape(shape6), tab["g"], tab["f"], tab["fi"], kf, tab["gi"],
      skip.reshape(1, D))
    return y.reshape(B, L, D)


def _filter_taps_kernel(feat_ref, t_ref, wf_ref, wb_ref, dl_ref, hf_ref, hb_ref, sum_ref, *, rows, total_rows):
    i = pl.program_id(1)
    window = jnp.exp(-t_ref[...] * dl_ref[...]) + HY_SHIFT
    feat = feat_ref[...].astype(BF16)
    hf = jnp.dot(feat, wf_ref[...].astype(BF16), preferred_element_type=F32) * window
    hb = jnp.dot(feat, wb_ref[...].astype(BF16), preferred_element_type=F32) * window
    hf_ref[...] = hf
    hb_ref[...] = hb
    row = lax.broadcasted_iota(jnp.int32, hb.shape, 0) + i * rows
    part = (jnp.sum(jnp.abs(hf), axis=0, keepdims=True)
            + jnp.sum(jnp.where(row < total_rows - 1, jnp.abs(hb), 0.0), axis=0, keepdims=True))

    @pl.when(i == 0)
    def _():
        sum_ref[...] = part

    @pl.when(i > 0)
    def _():
        sum_ref[...] += part


def _filter_taps(L, w1, b1, w2, b2, w3, b3, w_out, freq):
    hp = lax.Precision.HIGHEST
    D = w_out.shape[-1] // 2
    t = jnp.linspace(0.0, 1.0, L, dtype=F32)[:, None]
    w = (2.0 * math.pi / L) * jnp.arange(L, dtype=F32)[:, None]
    bands = jnp.linspace(1e-4, HY_BANDS - 1, HY_BANDS, dtype=F32)[None, :]
    z = jnp.concatenate([t, jnp.cos(bands * w), -jnp.sin(bands * w)], axis=-1)
    h = jnp.sin(freq * (jnp.dot(z, w1, precision=hp) + b1))
    h = jnp.sin(freq * (jnp.dot(h, w2, precision=hp) + b2))
    h = jnp.sin(freq * (jnp.dot(h, w3, precision=hp) + b3))
    max_decay = math.log(HY_TARGET) / HY_FAST_DECAY
    min_decay = math.log(HY_TARGET) / HY_SLOW_DECAY
    deltas = jnp.abs(jnp.linspace(min_decay, max_decay, D, dtype=F32))[None, :]
    rows, ct = min(512, L), 512
    width = h.shape[-1]
    nct = D // ct
    tap_spec = pl.BlockSpec((rows, ct), lambda j, i: (i, j))
    return pl.pallas_call(
        functools.partial(_filter_taps_kernel, rows=rows, total_rows=L),
        out_shape=(jax.ShapeDtypeStruct((L, D), F32), jax.ShapeDtypeStruct((L, D), F32),
                   jax.ShapeDtypeStruct((1, D), F32)),
        grid=(nct, L // rows),
        in_specs=[pl.BlockSpec((rows, width), lambda j, i: (i, 0)),
                  pl.BlockSpec((rows, 1), lambda j, i: (i, 0)),
                  pl.BlockSpec((width, ct), lambda j, i: (0, j)),
                  pl.BlockSpec((width, ct), lambda j, i: (0, nct + j)),
                  pl.BlockSpec((1, ct), lambda j, i: (0, j))],
        out_specs=(tap_spec, tap_spec, pl.BlockSpec((1, ct), lambda j, i: (0, j))),
        compiler_params=_cparams("parallel", "arbitrary"),
        name="hy_filter_taps",
    )(h, t, w_out, w_out, deltas)


def _hyena_mixer(h, x, g1, p, tab):
    B, L, D = x.shape
    z = _matmul(h.reshape(B * L, D), p["w_in"], layer=p["slot"], bias=p["b_in"], out_dtype=BF16,
                name="hy_in").reshape(B, L, 3 * D)
    x0, vv = _conv_gate(z, p["conv_w"], p["conv_b"])
    hf, hb, abs_sum = _filter_taps(L, p["f_w1"], p["f_b1"], p["f_w2"], p["f_b2"], p["f_w3"], p["f_b3"],
                                   p["f_wout"], p["freq"])
    y = _fft_conv(vv, x0, _filter_spectrum(hf, hb, abs_sum, tab), p["skip"], tab)
    out = _matmul(y.reshape(B * L, D), p["w_out"], layer=p["slot"], bias=p["b_out"],
                  resid=x.reshape(B * L, D), gate=g1, rows_per_gate=L, name="hy_out")
    return out.reshape(B, L, D)


STACKED_WEIGHTS = ("w_in", "w_out", "gate_w1")


def _layer_params(params, slot):
    p = {n: (v if n in STACKED_WEIGHTS else v[slot]) for n, v in params.items()}
    p["slot"] = slot
    return p


def _trunk(x, mod, hy, ret, gla, norm_g, mlp_w1, mlp_w2, final_g):
    B, L, D = x.shape
    tab = _dft_tables(L)
    tab["gf"] = _filter_tables(tab)
    for i in range(DEPTH):
        sh1, sc1, g1, sh2, sc2, g2 = jnp.split(mod[i], 6, axis=-1)
        h = _norm(x, norm_g[i, 0], sc1, sh1, out_dtype=BF16, name="norm_mix")
        kind, slot = i % N_MIXERS, i // N_MIXERS
        if kind == 0:
            x = _hyena_mixer(h, x, g1, _layer_params(hy, slot), tab)
        elif kind == 1:
            x = _retention_mixer(h, x, g1, _layer_params(ret, slot))
        else:
            x = _gla_mixer(h, x, g1, _layer_params(gla, slot))
        h = _norm(x, norm_g[i, 1], sc2, sh2, out_dtype=BF16, name="norm_mlp")
        a = _matmul(h.reshape(B * L, D), mlp_w1, layer=i, act="relu2", out_dtype=BF16, name="mlp_up")
        x = _matmul_deep(a, mlp_w2, i, x.reshape(B * L, D), g2, L, name="mlp_down").reshape(B, L, D)
    return _norm(x, final_g, out_dtype=F32, name="norm_final")


def kernel(x_prompt, x_sample, c_prompt, c_sample, hy_w_in, hy_b_in, hy_conv_w, hy_conv_b, hy_f_w1, hy_f_b1, hy_f_w2, hy_f_b2, hy_f_w3, hy_f_b3, hy_f_wout, hy_freq, hy_skip, hy_w_out, hy_b_out, ret_w_in, ret_w_out, gla_w_in, gla_gate_w1, gla_gate_w2, gla_gate_b, gla_w_out, norm_g, ada_w, ada_b, mlp_w1, mlp_w2, final_g):
    D = x_prompt.shape[-1]
    bf = lambda w: w.astype(BF16)
    hy = dict(w_in=bf(hy_w_in), b_in=hy_b_in, conv_w=hy_conv_w, conv_b=hy_conv_b,
              f_w1=hy_f_w1, f_b1=hy_f_b1, f_w2=hy_f_w2, f_b2=hy_f_b2, f_w3=hy_f_w3, f_b3=hy_f_b3,
              f_wout=hy_f_wout, freq=hy_freq, skip=hy_skip, w_out=bf(hy_w_out), b_out=hy_b_out)

    perm = _pair_split_perm(D, RET_HEADS)
    cols = np.concatenate([perm, D + perm, np.arange(2 * D, 4 * D)])
    ret = dict(w_in=bf(ret_w_in[:, :, cols]), w_out=bf(ret_w_out))

    n_gla, _, _, rank = gla_gate_w1.shape
    w1 = jnp.concatenate([gla_gate_w1[:, 0], gla_gate_w1[:, 1],
                          jnp.zeros((n_gla, D, V7X_LANES - 2 * rank), F32)], axis=-1)
    w2 = jnp.zeros((n_gla, 2, V7X_LANES, gla_gate_w2.shape[-1]), F32)
    w2 = w2.at[:, 0, :rank].set(gla_gate_w2[:, 0]).at[:, 1, rank:2 * rank].set(gla_gate_w2[:, 1])
    gla = dict(w_in=bf(gla_w_in), gate_w1=bf(w1), gate_w2=w2, gate_b=gla_gate_b[:, :, None, :],
               w_out=bf(gla_w_out))
    mw1, mw2 = bf(mlp_w1), bf(mlp_w2)

    nb_p, nb_s = c_prompt.shape[0], c_sample.shape[0]
    cs = jax.nn.silu(jnp.concatenate([c_prompt, c_sample], axis=0))
    rows = -(-cs.shape[0] // 16) * 16
    cs = jnp.pad(cs, ((0, rows - cs.shape[0]), (0, 0))).astype(BF16)
    mod = jnp.stack([_matmul(cs, ada_w, layer=i, bias=ada_b[i], bn=512, name="ada") for i in range(DEPTH)])
    mod_p, mod_s = mod[:, :nb_p], mod[:, nb_p:nb_p + nb_s]

    y_prompt = _trunk(x_prompt, mod_p, hy, ret, gla, norm_g, mw1, mw2, final_g)
    y_sample = _trunk(x_sample, mod_s, hy, ret, gla, norm_g, mw1, mw2, final_g)
    return (y_prompt, y_sample)
```

```python
import functools
import math

import jax
import jax.numpy as jnp
import numpy as np
from jax import lax
from jax.experimental import pallas as pl
from jax.experimental.pallas import tpu as pltpu

D_MODEL = 4096
DEPTH = 4
N_MIXERS = 3
EPS = 1e-6
HY_BANDS = 16
HY_FAST_DECAY = 0.3
HY_SLOW_DECAY = 1.5
HY_TARGET = 1e-2
HY_SHIFT = 0.05
RET_HEADS = 16
ROPE_BASE = 10000.0
GLA_HEADS = 4
GLA_GATE_RANK = 16
GLA_TAU = 16.0

F32 = jnp.float32
BF16 = jnp.bfloat16

V7X_VMEM_LIMIT_BYTES = 56 * 1024 * 1024
V7X_LANES = 128
MM_BLOCK_M = 1024
MM_BLOCK_N = 1024
MM_BLOCK_K = 4096
MM_DEEP_BLOCK_N = 512
MM_DEEP_BLOCK_K = 4096
MLP_DOWN_TILES = (dict(), dict(), dict(bm=512, bk=8192), dict(bm=512, bk=8192))
NORM_BLOCK_ROWS = 512
RET_CHUNK = 256
RET_HEADS_PER_STEP = 8
GLA_CHUNK = 128
GLA_SUBBLOCK = 32
GLA_EXP_CLAMP = 80.0


def _cparams(*sem):
    return pltpu.CompilerParams(dimension_semantics=sem, vmem_limit_bytes=V7X_VMEM_LIMIT_BYTES)


def _mm_kernel(*refs, nk, has_bias, has_resid, act):
    it = iter(refs)
    x_ref, w_ref = next(it), next(it)
    b_ref = next(it) if has_bias else None
    r_ref = next(it) if has_resid else None
    g_ref = next(it) if has_resid else None
    o_ref = next(it)
    acc_ref = next(it) if nk > 1 else None

    part = jnp.dot(x_ref[...].astype(BF16), w_ref[...].astype(BF16), preferred_element_type=F32)

    def finish(acc):
        if has_bias:
            acc = acc + b_ref[...]
        if act == "relu2":
            acc = jnp.maximum(acc, 0.0)
            acc = acc * acc
        if has_resid:
            acc = r_ref[...] + g_ref[...] * acc
        o_ref[...] = acc.astype(o_ref.dtype)

    if nk == 1:
        finish(part)
    else:
        k = pl.program_id(2)

        @pl.when(k == 0)
        def _():
            acc_ref[...] = part

        @pl.when(jnp.logical_and(k > 0, k < nk - 1))
        def _():
            acc_ref[...] += part

        @pl.when(k == nk - 1)
        def _():
            finish(acc_ref[...] + part)


def _matmul(x, w, *, layer=None, bias=None, act=None, resid=None, gate=None, rows_per_gate=None,
            out_dtype=F32, bm=MM_BLOCK_M, bn=MM_BLOCK_N, bk=MM_BLOCK_K, name="matmul"):
    M, K = x.shape
    K2, N = w.shape[-2:]
    assert K == K2
    bm, bn, bk = min(bm, M, rows_per_gate or M), min(bn, N), min(bk, K)
    assert M % bm == 0 and N % bn == 0 and K % bk == 0
    nk = K // bk
    has_bias = bias is not None
    has_resid = resid is not None

    operands = [x, w]
    in_specs = [pl.BlockSpec((bm, bk), lambda i, j, k: (i, k)),
                pl.BlockSpec((bk, bn), lambda i, j, k: (k, j)) if layer is None else
                pl.BlockSpec((None, bk, bn), lambda i, j, k: (layer, k, j))]
    if has_bias:
        operands.append(bias.reshape(1, N).astype(F32))
        in_specs.append(pl.BlockSpec((1, bn), lambda i, j, k: (0, j)))
    if has_resid:
        assert rows_per_gate % bm == 0
        tiles_per_gate = rows_per_gate // bm
        operands.append(resid)
        in_specs.append(pl.BlockSpec((bm, bn), lambda i, j, k: (i, j)))
        operands.append(gate.reshape(gate.shape[0], 1, N).astype(F32))
        in_specs.append(pl.BlockSpec((None, 1, bn), lambda i, j, k: (i // tiles_per_gate, 0, j)))

    scratch = [pltpu.VMEM((bm, bn), F32)] if nk > 1 else []
    kern = functools.partial(_mm_kernel, nk=nk, has_bias=has_bias, has_resid=has_resid, act=act)
    return pl.pallas_call(
        kern,
        out_shape=jax.ShapeDtypeStruct((M, N), out_dtype),
        grid=(M // bm, N // bn, nk),
        in_specs=in_specs,
        out_specs=pl.BlockSpec((bm, bn), lambda i, j, k: (i, j)),
        scratch_shapes=scratch,
        compiler_params=_cparams("parallel", "parallel", "arbitrary"),
        name=name,
    )(*operands)


def _mm_deep_kernel(x_ref, w_ref, r_ref, g_ref, o_ref, acc_ref, *, nk, bn):
    k = pl.program_id(1)
    cols = pl.ds(pl.multiple_of(pl.program_id(2) * bn, bn), bn)
    part = jnp.dot(x_ref[...], w_ref[...], preferred_element_type=F32)

    @pl.when(k == 0)
    def _():
        acc_ref[:, cols] = part

    @pl.when(jnp.logical_and(k > 0, k < nk - 1))
    def _():
        acc_ref[:, cols] += part

    @pl.when(k == nk - 1)
    def _():
        o_ref[...] = r_ref[...] + g_ref[...] * (acc_ref[:, cols] + part)


def _matmul_deep(x, w, layer, resid, gate, rows_per_gate, *, bm=MM_BLOCK_M, bn=MM_DEEP_BLOCK_N,
                 bk=MM_DEEP_BLOCK_K, name):
    M, K = x.shape
    N = w.shape[-1]
    bm = min(bm, M, rows_per_gate)
    assert M % bm == 0 and N % bn == 0 and K % bk == 0 and rows_per_gate % bm == 0
    nk = K // bk
    assert nk >= 2
    tiles_per_gate = rows_per_gate // bm
    jcol = lambda k, j: jnp.where(k == nk - 1, j, 0)
    return pl.pallas_call(
        functools.partial(_mm_deep_kernel, nk=nk, bn=bn),
        out_shape=jax.ShapeDtypeStruct((M, N), F32),
        grid=(M // bm, nk, N // bn),
        in_specs=[pl.BlockSpec((bm, bk), lambda i, k, j: (i, k)),
                  pl.BlockSpec((None, bk, bn), lambda i, k, j: (layer, k, j)),
                  pl.BlockSpec((bm, bn), lambda i, k, j: (i, jcol(k, j))),
                  pl.BlockSpec((None, 1, bn), lambda i, k, j: (i // tiles_per_gate, 0, jcol(k, j)))],
        out_specs=pl.BlockSpec((bm, bn), lambda i, k, j: (i, jcol(k, j))),
        scratch_shapes=[pltpu.VMEM((bm, N), F32)],
        compiler_params=_cparams("parallel", "arbitrary", "arbitrary"),
        name=name,
    )(x, w, resid, gate.reshape(gate.shape[0], 1, N).astype(F32))


def _norm_kernel(x_ref, g_ref, *rest, modulate):
    if modulate:
        sc_ref, sh_ref, o_ref = rest
    else:
        (o_ref,) = rest
    x = x_ref[...]
    y = x * lax.rsqrt(jnp.mean(x * x, axis=-1, keepdims=True) + EPS) * g_ref[...]
    if modulate:
        y = y * (1.0 + sc_ref[...]) + sh_ref[...]
    o_ref[...] = y.astype(o_ref.dtype)


def _norm(x, g, scale=None, shift=None, *, out_dtype, name):
    B, L, D = x.shape
    rows = min(NORM_BLOCK_ROWS, L)
    assert L % rows == 0
    modulate = scale is not None
    operands = [x, g.reshape(1, D)]
    in_specs = [pl.BlockSpec((None, rows, D), lambda b, i: (b, i, 0)),
                pl.BlockSpec((1, D), lambda b, i: (0, 0))]
    if modulate:
        operands += [scale.reshape(B, 1, D), shift.reshape(B, 1, D)]
        in_specs += [pl.BlockSpec((None, 1, D), lambda b, i: (b, 0, 0))] * 2
    return pl.pallas_call(
        functools.partial(_norm_kernel, modulate=modulate),
        out_shape=jax.ShapeDtypeStruct((B, L, D), out_dtype),
        grid=(B, L // rows),
        in_specs=in_specs,
        out_specs=pl.BlockSpec((None, rows, D), lambda b, i: (b, i, 0)),
        compiler_params=_cparams("parallel", "parallel"),
        name=name,
    )(*operands)


def _silu(x):
    return x / (1.0 + jnp.exp(-x))


def _gated_head_norm(o, gate):
    o = o * lax.rsqrt(jnp.mean(o * o, axis=-1, keepdims=True) + EPS)
    return _silu(gate) * o


def _ret_kernel(lg_ref, q_ref, k_ref, v_ref, cos_ref, sin_ref, *rest, reverse, final, heads_per_step, hd):
    if final:
        op_ref, g_ref, o_ref, state_ref = rest
    else:
        o_ref, state_ref = rest
    C = q_ref.shape[0]
    half = hd // 2

    @pl.when(pl.program_id(2) == 0)
    def _():
        state_ref[...] = jnp.zeros_like(state_ref)

    cos, sin = cos_ref[...], sin_ref[...]
    row = lax.broadcasted_iota(jnp.int32, (C, C), 0)
    col = lax.broadcasted_iota(jnp.int32, (C, C), 1)
    ridx = lax.broadcasted_iota(jnp.int32, (C, 1), 0).astype(F32)
    if reverse:
        mask = col > row
        dist = (col - row).astype(F32)
        q_pow, k_pow = C - ridx, ridx
    else:
        mask = col <= row
        dist = (row - col).astype(F32)
        q_pow, k_pow = ridx + 1.0, C - 1.0 - ridx
    dist = jnp.where(mask, dist, 0.0)

    def rotate(x):
        x1, x2 = x[:, :half], x[:, half:]
        return jnp.concatenate([x1 * cos - x2 * sin, x1 * sin + x2 * cos], axis=1)

    for hh in range(heads_per_step):
        lg = lg_ref[pl.program_id(1) * heads_per_step + hh]
        sl = slice(hh * hd, (hh + 1) * hd)
        q = rotate(q_ref[:, sl].astype(F32))
        k = rotate(k_ref[:, sl].astype(F32)) * (hd ** -0.5)
        v = v_ref[:, sl].astype(BF16)
        intra = jnp.where(mask, jnp.exp(lg * dist), 0.0)
        s = lax.dot_general(q.astype(BF16), k.astype(BF16), (((1,), (1,)), ((), ())),
                            preferred_element_type=F32) * intra
        state = state_ref[hh]
        o = (jnp.dot(s.astype(BF16), v, preferred_element_type=F32)
             + jnp.dot((q * jnp.exp(lg * q_pow)).astype(BF16), state.astype(BF16),
                       preferred_element_type=F32))
        kd = (k * jnp.exp(lg * k_pow)).astype(BF16)
        state_ref[hh] = state * jnp.exp(lg * jnp.full((1, hd), C, F32)) + lax.dot_general(
            kd, v, (((0,), (0,)), ((), ())), preferred_element_type=F32)
        if final:
            o_ref[:, sl] = _gated_head_norm(op_ref[:, sl].astype(F32) + o,
                                            g_ref[:, sl].astype(F32)).astype(o_ref.dtype)
        else:
            o_ref[:, sl] = o.astype(o_ref.dtype)


def _ret_pass(z, log_gamma, cos, sin, o_prev, *, reverse):
    B, L, D4 = z.shape
    D = D4 // 4
    hd = D // RET_HEADS
    hps = RET_HEADS_PER_STEP
    W = hps * hd
    C = min(RET_CHUNK, L)
    n = L // C
    nblk = D // W
    final = o_prev is not None
    cidx = (lambda c: n - 1 - c) if reverse else (lambda c: c)

    def zspec(part):
        return pl.BlockSpec((None, C, W), lambda b, h, c, lg: (b, cidx(c), part * nblk + h))

    tspec = pl.BlockSpec((C, hd // 2), lambda b, h, c, lg: (cidx(c), 0))
    ospec = pl.BlockSpec((None, C, W), lambda b, h, c, lg: (b, cidx(c), h))
    operands = [z, z, z, cos, sin]
    in_specs = [zspec(0), zspec(1), zspec(2), tspec, tspec]
    if final:
        operands += [o_prev, z]
        in_specs += [ospec, zspec(3)]
    kern = functools.partial(_ret_kernel, reverse=reverse, final=final, heads_per_step=hps, hd=hd)
    return pl.pallas_call(
        kern,
        out_shape=jax.ShapeDtypeStruct((B, L, D), BF16),
        grid_spec=pltpu.PrefetchScalarGridSpec(
            num_scalar_prefetch=1,
            grid=(B, nblk, n),
            in_specs=in_specs,
            out_specs=ospec,
            scratch_shapes=[pltpu.VMEM((hps, hd, hd), F32)]),
        compiler_params=_cparams("parallel", "parallel", "arbitrary"),
        name="ret_bwd" if reverse else "ret_fwd",
    )(log_gamma, *operands)


def _pair_split_perm(d, heads):
    hd = d // heads
    idx = np.arange(d).reshape(heads, hd)
    return np.concatenate([idx[:, 0::2], idx[:, 1::2]], axis=1).reshape(-1)


def _retention_mixer(h, x, g1, p):
    B, L, D = x.shape
    hd = D // RET_HEADS
    z = _matmul(h.reshape(B * L, D), p["w_in"], layer=p["slot"], out_dtype=BF16,
                name="ret_in").reshape(B, L, 4 * D)
    pos = jnp.arange(L, dtype=F32)
    inv = 1.0 / (ROPE_BASE ** jnp.linspace(0.0, 1.0, hd // 2, dtype=F32))
    ang = pos[:, None] * inv[None, :]
    cos, sin = jnp.cos(ang), jnp.sin(ang)
    hidx = jnp.arange(RET_HEADS, dtype=F32)
    lg_fwd = jnp.log1p(-jnp.exp2(-5.0 - hidx))
    lg_bwd = jnp.log1p(-jnp.exp2(-5.0 - hidx[::-1]))
    o_fwd = _ret_pass(z, lg_fwd, cos, sin, None, reverse=False)
    y = _ret_pass(z, lg_bwd, cos, sin, o_fwd, reverse=True)
    out = _matmul(y.reshape(B * L, D), p["w_out"], layer=p["slot"], resid=x.reshape(B * L, D), gate=g1,
                  rows_per_gate=L, name="ret_out")
    return out.reshape(B, L, D)


def _gla_kernel(q_ref, k_ref, v_ref, t_ref, w2_ref, gb_ref, *rest, reverse, final, sub):
    if final:
        op_ref, r_ref, o_ref, state_ref, a_ref = rest
    else:
        o_ref, state_ref, a_ref = rest
    C, dk = q_ref.shape

    @pl.when(pl.program_id(2) == 0)
    def _():
        state_ref[...] = jnp.zeros_like(state_ref)

    zg = jnp.dot(t_ref[...].astype(BF16), w2_ref[...].astype(BF16), preferred_element_type=F32) + gb_ref[...]
    log_a = (jnp.minimum(zg, 0.0) - jnp.log1p(jnp.exp(-jnp.abs(zg)))) * (1.0 / GLA_TAU)

    row = lax.broadcasted_iota(jnp.int32, (C, C), 0)
    col = lax.broadcasted_iota(jnp.int32, (C, C), 1)
    tri = ((col >= row) if reverse else (col <= row)).astype(BF16)
    hi = log_a.astype(BF16)
    lo = (log_a - hi.astype(F32)).astype(BF16)
    b = (jnp.dot(tri, hi, preferred_element_type=F32) + jnp.dot(tri, lo, preferred_element_type=F32))
    b_tot = jnp.sum(log_a, axis=0, keepdims=True)

    q = q_ref[...].astype(F32) * (dk ** -0.5)
    k = k_ref[...].astype(F32)
    v = v_ref[...].astype(BF16)

    for blk in range(C // sub):
        rows = slice(blk * sub, (blk + 1) * sub)
        ref = b[blk * sub + sub // 2:blk * sub + sub // 2 + 1, :]
        qh = q[rows] * jnp.exp(jnp.minimum(b[rows] - ref, GLA_EXP_CLAMP))
        kh = k * jnp.exp(jnp.minimum(ref - b, GLA_EXP_CLAMP))
        s = lax.dot_general(qh.astype(BF16), kh.astype(BF16), (((1,), (1,)), ((), ())),
                            preferred_element_type=F32)
        r_i = lax.broadcasted_iota(jnp.int32, (sub, C), 0) + blk * sub
        c_i = lax.broadcasted_iota(jnp.int32, (sub, C), 1)
        keep = (c_i > r_i) if reverse else (c_i <= r_i)
        a_ref[rows, :] = jnp.where(keep, s, 0.0).astype(BF16)

    state = state_ref[...]
    o = (jnp.dot(a_ref[...], v, preferred_element_type=F32)
         + lax.dot_general((q * jnp.exp(b)).astype(BF16), state.astype(BF16), (((1,), (1,)), ((), ())),
                           preferred_element_type=F32))
    kd = (k * jnp.exp(b_tot - b)).astype(BF16)
    state_ref[...] = state * jnp.exp(b_tot) + lax.dot_general(
        v, kd, (((0,), (0,)), ((), ())), preferred_element_type=F32)
    if final:
        o_ref[...] = _gated_head_norm(op_ref[...].astype(F32) + o, r_ref[...].astype(F32)).astype(o_ref.dtype)
    else:
        o_ref[...] = o.astype(o_ref.dtype)


def _gla_pass(z, t, w2, gb, o_prev, *, direction):
    B, L, D3 = z.shape
    D = D3 // 3
    dk = D // 2 // GLA_HEADS
    dv = D // GLA_HEADS
    C = min(GLA_CHUNK, L)
    n = L // C
    reverse = direction == 1
    final = o_prev is not None
    cidx = (lambda c: n - 1 - c) if reverse else (lambda c: c)
    H = GLA_HEADS

    qspec = pl.BlockSpec((None, C, dk), lambda b, h, c: (b, cidx(c), h))
    kspec = pl.BlockSpec((None, C, dk), lambda b, h, c: (b, cidx(c), H + h))
    vspec = pl.BlockSpec((None, C, dv), lambda b, h, c: (b, cidx(c), H + h))
    rspec = pl.BlockSpec((None, C, dv), lambda b, h, c: (b, cidx(c), 2 * H + h))
    ospec = pl.BlockSpec((None, C, dv), lambda b, h, c: (b, cidx(c), h))
    tspec = pl.BlockSpec((None, C, t.shape[-1]), lambda b, h, c: (b, cidx(c), 0))
    w2spec = pl.BlockSpec((None, t.shape[-1], dk), lambda b, h, c: (direction, 0, h))
    gbspec = pl.BlockSpec((None, 1, dk), lambda b, h, c: (direction, 0, h))
    operands = [z, z, z, t, w2, gb]
    in_specs = [qspec, kspec, vspec, tspec, w2spec, gbspec]
    if final:
        operands += [o_prev, z]
        in_specs += [ospec, rspec]
    kern = functools.partial(_gla_kernel, reverse=reverse, final=final, sub=min(GLA_SUBBLOCK, C))
    return pl.pallas_call(
        kern,
        out_shape=jax.ShapeDtypeStruct((B, L, D), BF16),
        grid=(B, H, n),
        in_specs=in_specs,
        out_specs=ospec,
        scratch_shapes=[pltpu.VMEM((dv, dk), F32), pltpu.VMEM((C, C), BF16)],
        compiler_params=_cparams("parallel", "parallel", "arbitrary"),
        name="gla_bwd" if reverse else "gla_fwd",
    )(*operands)


def _gla_mixer(h, x, g1, p):
    B, L, D = x.shape
    h2 = h.reshape(B * L, D)
    z = _matmul(h2, p["w_in"], layer=p["slot"], out_dtype=BF16, name="gla_in").reshape(B, L, 3 * D)
    t = _matmul(h2, p["gate_w1"], layer=p["slot"], bn=V7X_LANES, name="gla_gate").reshape(B, L, V7X_LANES)
    o_fwd = _gla_pass(z, t, p["gate_w2"], p["gate_b"], None, direction=0)
    y = _gla_pass(z, t, p["gate_w2"], p["gate_b"], o_fwd, direction=1)
    out = _matmul(y.reshape(B * L, D), p["w_out"], layer=p["slot"], resid=x.reshape(B * L, D), gate=g1,
                  rows_per_gate=L, name="gla_out")
    return out.reshape(B, L, D)


CONV_HALO = 16


def _conv_gate_kernel(*refs, rows):
    groups = [refs[5 * g:5 * g + 5] for g in range(3)]
    x0_out, vv_out = refs[15:]
    i = pl.program_id(1)
    first = i == 0
    last = i == pl.num_programs(1) - 1
    h = CONV_HALO
    ct = groups[0][0].shape[1]
    row = lax.broadcasted_iota(jnp.int32, (rows, ct), 0)

    def put(out_ref, val):
        lanes = out_ref.shape[-1]
        for t in range(out_ref.shape[0]):
            out_ref[t] = val[:, t * lanes:(t + 1) * lanes].astype(out_ref.dtype)

    def conv(cur_ref, prev_ref, next_ref, w_ref, b_ref):
        cur = cur_ref[...].astype(F32)
        before = jnp.where(first, 0.0, prev_ref[h - 1:h, :].astype(F32))
        after = jnp.where(last, 0.0, next_ref[0:1, :].astype(F32))
        prev = jnp.where(row == 0, before, pltpu.roll(cur, 1, axis=0))
        nxt = jnp.where(row == rows - 1, after, pltpu.roll(cur, rows - 1, axis=0))
        w = w_ref[...]
        return prev * w[0:1] + cur * w[1:2] + nxt * w[2:3] + b_ref[...]

    put(x0_out, conv(*groups[0]))
    x1 = conv(*groups[1])
    put(vv_out, conv(*groups[2]) * x1)


def _conv_gate(z, conv_w, conv_b):
    B, L, D3 = z.shape
    D = D3 // 3
    rows, ct = min(512, L), 512
    nct = D // ct
    h = CONV_HALO
    nhalo = L // h
    operands, in_specs = [], []
    for g in range(3):
        operands += [z, z, z, conv_w, conv_b.reshape(1, D3)]
        in_specs += [
            pl.BlockSpec((None, rows, ct), lambda b, i, j, g=g: (b, i, g * nct + j)),
            pl.BlockSpec((None, h, ct), lambda b, i, j, g=g: (b, jnp.maximum(i * (rows // h) - 1, 0), g * nct + j)),
            pl.BlockSpec((None, h, ct), lambda b, i, j, g=g: (b, jnp.minimum((i + 1) * (rows // h), nhalo - 1), g * nct + j)),
            pl.BlockSpec((3, ct), lambda b, i, j, g=g: (0, g * nct + j)),
            pl.BlockSpec((1, ct), lambda b, i, j, g=g: (0, g * nct + j)),
        ]
    tiles = ct // FFT_LANES
    ospec = pl.BlockSpec((None, tiles, rows, FFT_LANES), lambda b, i, j: (b, j, i, 0))
    oshape = jax.ShapeDtypeStruct((B, D // FFT_LANES, L, FFT_LANES), BF16)
    return pl.pallas_call(
        functools.partial(_conv_gate_kernel, rows=rows),
        out_shape=(oshape, oshape),
        grid=(B, L // rows, nct),
        in_specs=in_specs,
        out_specs=(ospec, ospec),
        compiler_params=_cparams("parallel", "parallel", "parallel"),
        name="hy_conv_gate",
    )(*operands)


FFT_GROUP = 16
FFT_K1_GROUP = 16
FFT_LANES = 256


def _stage_a(x_parts, g_ref, z_ref, stage_ref, step, *, n1, ng, mirror_second=False):
    xt = [jnp.swapaxes(x, 0, 1) for x in x_parts]
    for j in range(FFT_GROUP):
        x = jnp.concatenate([xt[0][j], xt[1][FFT_GROUP - 1 - j if mirror_second else j]], axis=0)
        stage_ref[j] = jnp.dot(g_ref[j], x.astype(BF16), preferred_element_type=F32)
    a = jnp.swapaxes(stage_ref[...], 0, 1)
    z_ref[:, step] = a[:n1].astype(z_ref.dtype)
    z_ref[:, ng + step] = a[n1:].astype(z_ref.dtype)


def _fft_conv_kernel(x_ref, x0_ref, g_ref, f_ref, fi_ref, kf_ref, gi_ref, skip_ref, o_ref, z_ref, sa_ref, sc_ref,
                     *, sa, sb, n1, n2, h1):
    step = pl.program_id(2)
    ng = n2 // FFT_GROUP

    @pl.when(step < sa)
    def _():
        _stage_a([x_ref[0].astype(F32), x_ref[1].astype(F32)], g_ref, z_ref, sa_ref, step, n1=n1, ng=ng)

    @pl.when(jnp.logical_and(step >= sa, step < sa + sb))
    def _():
        for j in range(FFT_K1_GROUP):
            k1 = (step - sa) * FFT_K1_GROUP + j
            zk = z_ref[k1].reshape(2 * n2, z_ref.shape[-1])
            y = jnp.dot(f_ref[...], zk, preferred_element_type=F32)
            yr, yi = y[:n2], y[n2:]
            kr, ki = kf_ref[j, :n2, :], kf_ref[j, n2:, :]
            p = jnp.concatenate([yr * kr - yi * ki, yr * ki + yi * kr], axis=0).astype(BF16)
            back = jnp.dot(fi_ref[...], p, preferred_element_type=F32)
            z_ref[k1] = back.astype(z_ref.dtype).reshape(z_ref.shape[1:])

    @pl.when(step >= sa + sb)
    def _():
        g = step - sa - sb
        zr = jnp.swapaxes(z_ref[:, g].astype(F32), 0, 1)
        zi = jnp.swapaxes(z_ref[:, ng + g].astype(F32), 0, 1)
        for j in range(FFT_GROUP):
            bz = jnp.concatenate([zr[j], zi[j]], axis=0).astype(BF16)
            sc_ref[j] = jnp.dot(gi_ref[j], bz, preferred_element_type=F32)
        y = jnp.swapaxes(sc_ref[...], 0, 1).reshape(x_ref.shape)
        o_ref[...] = ((y + skip_ref[...] * x_ref[...].astype(F32)) * x0_ref[...].astype(F32)).astype(o_ref.dtype)


def _dft_tables(L):
    N = 2 * L
    n1 = n2 = int(round(math.sqrt(N)))
    assert n1 * n2 == N and n2 % FFT_GROUP == 0 and n1 % FFT_K1_GROUP == 0
    h1 = n1 // 2
    k1 = jnp.arange(n1, dtype=jnp.int32)
    m = (jnp.arange(n2, dtype=jnp.int32)[:, None, None] * k1[None, :, None]
         + n2 * k1[None, :, None] * jnp.arange(h1, dtype=jnp.int32)[None, None, :]) % N
    ang = (2.0 * math.pi / N) * m.astype(F32)
    c, s = jnp.cos(ang), jnp.sin(ang)
    block = lambda re, im: jnp.concatenate([jnp.concatenate([re, -im], -1), jnp.concatenate([im, re], -1)], -2)
    g = block(c, -s).astype(BF16)
    ct_, st_ = jnp.swapaxes(c, 1, 2) / N, jnp.swapaxes(s, 1, 2) / N
    gi = block(ct_, st_).astype(BF16)
    idx = jnp.arange(n2, dtype=jnp.int32)
    ang2 = (2.0 * math.pi / n2) * ((idx[:, None] * idx[None, :]) % n2).astype(F32)
    c2, s2 = jnp.cos(ang2), jnp.sin(ang2)
    return dict(n1=n1, n2=n2, h1=h1, g=g, gi=gi, f=block(c2, -s2).astype(BF16), fi=block(c2, s2).astype(BF16))


def _filter_spectrum_kernel(hf_ref, hb_ref, g_ref, f_ref, sum_ref, o_ref, z_ref, sa_ref, *, sa, n1, n2):
    step = pl.program_id(1)
    ng = n2 // FFT_GROUP

    @pl.when(step < sa)
    def _():
        _stage_a([hf_ref[...], hb_ref[...]], g_ref, z_ref, sa_ref, step, n1=n1, ng=ng, mirror_second=True)

    @pl.when(step >= sa)
    def _():
        scale = 1.0 / (sum_ref[...] + EPS)
        for j in range(FFT_K1_GROUP):
            k1 = (step - sa) * FFT_K1_GROUP + j
            zk = z_ref[k1].reshape(2 * n2, z_ref.shape[-1])
            o_ref[j] = (jnp.dot(f_ref[...], zk, preferred_element_type=F32) * scale).astype(o_ref.dtype)


def _filter_tables(tab):
    n1, n2, h1 = tab["n1"], tab["n2"], tab["h1"]
    N = n1 * n2
    k1 = jnp.arange(n1, dtype=jnp.int32)[None, :, None]
    i2 = jnp.arange(n2, dtype=jnp.int32)[:, None, None]
    a = jnp.arange(h1, dtype=jnp.int32)[None, None, :]
    rows = jnp.concatenate([a, n1 - 1 - a], axis=-1)
    m = (i2 * k1 + n2 * k1 * rows) % N
    ang = (2.0 * math.pi / N) * m.astype(F32)
    col = jnp.arange(2 * h1, dtype=jnp.int32)[None, None, :]
    live = jnp.logical_not(jnp.logical_and(i2 == 0, col == 2 * h1 - 1))
    re = jnp.where(live, jnp.cos(ang), 0.0)
    im = jnp.where(live, -jnp.sin(ang), 0.0)
    return jnp.concatenate([re, im], axis=1).astype(BF16)


def _filter_spectrum(hf, hb, abs_sum, tab):
    L, D = hf.shape
    n1, n2, h1 = tab["n1"], tab["n2"], tab["h1"]
    ct = FFT_LANES
    sa = n2 // FFT_GROUP
    sb = n1 // FFT_K1_GROUP
    kern = functools.partial(_filter_spectrum_kernel, sa=sa, n1=n1, n2=n2)
    return pl.pallas_call(
        kern,
        out_shape=jax.ShapeDtypeStruct((n1, 2 * n2, D), BF16),
        grid=(D // ct, sa + sb),
        in_specs=[
            pl.BlockSpec((h1, FFT_GROUP, ct), lambda t, s: (0, jnp.minimum(s, sa - 1), t)),
            pl.BlockSpec((h1, FFT_GROUP, ct), lambda t, s: (0, sa - 1 - jnp.minimum(s, sa - 1), t)),
            pl.BlockSpec((FFT_GROUP, 2 * n1, 2 * h1), lambda t, s: (jnp.minimum(s, sa - 1), 0, 0)),
            pl.BlockSpec((2 * n2, 2 * n2), lambda t, s: (0, 0)),
            pl.BlockSpec((1, ct), lambda t, s: (0, t)),
        ],
        out_specs=pl.BlockSpec((FFT_K1_GROUP, 2 * n2, ct), lambda t, s: (jnp.maximum(s - sa, 0), 0, t)),
        scratch_shapes=[pltpu.VMEM((n1, 2 * n2 // FFT_GROUP, FFT_GROUP, ct), BF16),
                        pltpu.VMEM((FFT_GROUP, 2 * n1, ct), F32)],
        compiler_params=_cparams("parallel", "arbitrary"),
        name="hy_filter_spectrum",
    )(hf.reshape(h1, n2, D), hb.reshape(h1, n2, D), tab["gf"], tab["f"], abs_sum)


def _fft_conv(vv, x0, kf, skip, tab):
    B, ntile, L, ct = vv.shape
    D = ntile * ct
    n1, n2, h1 = tab["n1"], tab["n2"], tab["h1"]
    assert B % 2 == 0 and L == h1 * n2 and ct == FFT_LANES
    sa = n2 // FFT_GROUP
    sb = n1 // FFT_K1_GROUP
    sc = sa
    shape5 = (B // 2, 2, h1, n2, D)
    shape6 = (B // 2, 2, ntile, h1, n2, ct)
    a_idx = lambda s: jnp.where(s < sa, s, jnp.where(s >= sa + sb, s - sa - sb, sa - 1))
    c_idx = lambda s: jnp.clip(s - sa - sb, 0, sc - 1)
    blk = (None, 2, h1, FFT_GROUP, ct)
    blk6 = (None, 2, None, h1, FFT_GROUP, ct)
    in_specs = [
        pl.BlockSpec(blk6, lambda p, t, s: (p, 0, t, 0, a_idx(s), 0)),
        pl.BlockSpec(blk6, lambda p, t, s: (p, 0, t, 0, c_idx(s), 0)),
        pl.BlockSpec((FFT_GROUP, 2 * n1, 2 * h1), lambda p, t, s: (jnp.minimum(s, sa - 1), 0, 0)),
        pl.BlockSpec((2 * n2, 2 * n2), lambda p, t, s: (0, 0)),
        pl.BlockSpec((2 * n2, 2 * n2), lambda p, t, s: (0, 0)),
        pl.BlockSpec((FFT_K1_GROUP, 2 * n2, ct), lambda p, t, s: (jnp.clip(s - sa, 0, sb - 1), 0, t)),
        pl.BlockSpec((FFT_GROUP, 2 * h1, 2 * n1), lambda p, t, s: (c_idx(s), 0, 0)),
        pl.BlockSpec((1, ct), lambda p, t, s: (0, t)),
    ]
    kern = functools.partial(_fft_conv_kernel, sa=sa, sb=sb, n1=n1, n2=n2, h1=h1)
    y = pl.pallas_call(
        kern,
        out_shape=jax.ShapeDtypeStruct(shape5, BF16),
        grid=(B // 2, D // ct, sa + sb + sc),
        in_specs=in_specs,
        out_specs=pl.BlockSpec(blk, lambda p, t, s: (p, 0, 0, c_idx(s), t)),
        scratch_shapes=[pltpu.VMEM((n1, 2 * n2 // FFT_GROUP, FFT_GROUP, ct), BF16),
                        pltpu.VMEM((FFT_GROUP, 2 * n1, ct), F32),
                        pltpu.VMEM((FFT_GROUP, 2 * h1, ct), F32)],
        compiler_params=_cparams("parallel", "parallel", "arbitrary"),
        name="hy_fft_conv",
    )(vv.reshape(shape6), x0.reshape(shape6), tab["g"], tab["f"], tab["fi"], kf, tab["gi"],
      skip.reshape(1, D))
    return y.reshape(B, L, D)


def _filter_taps_kernel(feat_ref, t_ref, wf_ref, wb_ref, dl_ref, hf_ref, hb_ref, sum_ref, *, rows, total_rows):
    i = pl.program_id(1)
    window = jnp.exp(-t_ref[...] * dl_ref[...]) + HY_SHIFT
    feat = feat_ref[...].astype(BF16)
    hf = jnp.dot(feat, wf_ref[...].astype(BF16), preferred_element_type=F32) * window
    hb = jnp.dot(feat, wb_ref[...].astype(BF16), preferred_element_type=F32) * window
    hf_ref[...] = hf
    hb_ref[...] = hb
    row = lax.broadcasted_iota(jnp.int32, hb.shape, 0) + i * rows
    part = (jnp.sum(jnp.abs(hf), axis=0, keepdims=True)
            + jnp.sum(jnp.where(row < total_rows - 1, jnp.abs(hb), 0.0), axis=0, keepdims=True))

    @pl.when(i == 0)
    def _():
        sum_ref[...] = part

    @pl.when(i > 0)
    def _():
        sum_ref[...] += part


def _filter_taps(L, w1, b1, w2, b2, w3, b3, w_out, freq):
    hp = lax.Precision.HIGHEST
    D = w_out.shape[-1] // 2
    t = jnp.linspace(0.0, 1.0, L, dtype=F32)[:, None]
    w = (2.0 * math.pi / L) * jnp.arange(L, dtype=F32)[:, None]
    bands = jnp.linspace(1e-4, HY_BANDS - 1, HY_BANDS, dtype=F32)[None, :]
    z = jnp.concatenate([t, jnp.cos(bands * w), -jnp.sin(bands * w)], axis=-1)
    h = jnp.sin(freq * (jnp.dot(z, w1, precision=hp) + b1))
    h = jnp.sin(freq * (jnp.dot(h, w2, precision=hp) + b2))
    h = jnp.sin(freq * (jnp.dot(h, w3, precision=hp) + b3))
    max_decay = math.log(HY_TARGET) / HY_FAST_DECAY
    min_decay = math.log(HY_TARGET) / HY_SLOW_DECAY
    deltas = jnp.abs(jnp.linspace(min_decay, max_decay, D, dtype=F32))[None, :]
    rows, ct = min(512, L), 512
    width = h.shape[-1]
    nct = D // ct
    tap_spec = pl.BlockSpec((rows, ct), lambda j, i: (i, j))
    return pl.pallas_call(
        functools.partial(_filter_taps_kernel, rows=rows, total_rows=L),
        out_shape=(jax.ShapeDtypeStruct((L, D), F32), jax.ShapeDtypeStruct((L, D), F32),
                   jax.ShapeDtypeStruct((1, D), F32)),
        grid=(nct, L // rows),
        in_specs=[pl.BlockSpec((rows, width), lambda j, i: (i, 0)),
                  pl.BlockSpec((rows, 1), lambda j, i: (i, 0)),
                  pl.BlockSpec((width, ct), lambda j, i: (0, j)),
                  pl.BlockSpec((width, ct), lambda j, i: (0, nct + j)),
                  pl.BlockSpec((1, ct), lambda j, i: (0, j))],
        out_specs=(tap_spec, tap_spec, pl.BlockSpec((1, ct), lambda j, i: (0, j))),
        compiler_params=_cparams("parallel", "arbitrary"),
        name="hy_filter_taps",
    )(h, t, w_out, w_out, deltas)


def _hyena_mixer(h, x, g1, p, tab):
    B, L, D = x.shape
    z = _matmul(h.reshape(B * L, D), p["w_in"], layer=p["slot"], bias=p["b_in"], out_dtype=BF16,
                name="hy_in").reshape(B, L, 3 * D)
    x0, vv = _conv_gate(z, p["conv_w"], p["conv_b"])
    hf, hb, abs_sum = _filter_taps(L, p["f_w1"], p["f_b1"], p["f_w2"], p["f_b2"], p["f_w3"], p["f_b3"],
                                   p["f_wout"], p["freq"])
    y = _fft_conv(vv, x0, _filter_spectrum(hf, hb, abs_sum, tab), p["skip"], tab)
    out = _matmul(y.reshape(B * L, D), p["w_out"], layer=p["slot"], bias=p["b_out"],
                  resid=x.reshape(B * L, D), gate=g1, rows_per_gate=L, name="hy_out")
    return out.reshape(B, L, D)


STACKED_WEIGHTS = ("w_in", "w_out", "gate_w1")


def _layer_params(params, slot):
    p = {n: (v if n in STACKED_WEIGHTS else v[slot]) for n, v in params.items()}
    p["slot"] = slot
    return p


def _trunk(x, mod, hy, ret, gla, norm_g, mlp_w1, mlp_w2, final_g):
    B, L, D = x.shape
    tab = _dft_tables(L)
    tab["gf"] = _filter_tables(tab)
    for i in range(DEPTH):
        sh1, sc1, g1, sh2, sc2, g2 = jnp.split(mod[i], 6, axis=-1)
        h = _norm(x, norm_g[i, 0], sc1, sh1, out_dtype=BF16, name="norm_mix")
        kind, slot = i % N_MIXERS, i // N_MIXERS
        if kind == 0:
            x = _hyena_mixer(h, x, g1, _layer_params(hy, slot), tab)
        elif kind == 1:
            x = _retention_mixer(h, x, g1, _layer_params(ret, slot))
        else:
            x = _gla_mixer(h, x, g1, _layer_params(gla, slot))
        h = _norm(x, norm_g[i, 1], sc2, sh2, out_dtype=BF16, name="norm_mlp")
        a = _matmul(h.reshape(B * L, D), mlp_w1, layer=i, act="relu2", out_dtype=BF16, name="mlp_up")
        x = _matmul_deep(a, mlp_w2, i, x.reshape(B * L, D), g2, L, name="mlp_down",
                         **MLP_DOWN_TILES[i]).reshape(B, L, D)
    return _norm(x, final_g, out_dtype=F32, name="norm_final")


def kernel(x_prompt, x_sample, c_prompt, c_sample, hy_w_in, hy_b_in, hy_conv_w, hy_conv_b, hy_f_w1, hy_f_b1, hy_f_w2, hy_f_b2, hy_f_w3, hy_f_b3, hy_f_wout, hy_freq, hy_skip, hy_w_out, hy_b_out, ret_w_in, ret_w_out, gla_w_in, gla_gate_w1, gla_gate_w2, gla_gate_b, gla_w_out, norm_g, ada_w, ada_b, mlp_w1, mlp_w2, final_g):
    D = x_prompt.shape[-1]
    bf = lambda w: w.astype(BF16)
    hy = dict(w_in=bf(hy_w_in), b_in=hy_b_in, conv_w=hy_conv_w, conv_b=hy_conv_b,
              f_w1=hy_f_w1, f_b1=hy_f_b1, f_w2=hy_f_w2, f_b2=hy_f_b2, f_w3=hy_f_w3, f_b3=hy_f_b3,
              f_wout=hy_f_wout, freq=hy_freq, skip=hy_skip, w_out=bf(hy_w_out), b_out=hy_b_out)

    perm = _pair_split_perm(D, RET_HEADS)
    cols = np.concatenate([perm, D + perm, np.arange(2 * D, 4 * D)])
    ret = dict(w_in=bf(ret_w_in[:, :, cols]), w_out=bf(ret_w_out))

    n_gla, _, _, rank = gla_gate_w1.shape
    w1 = jnp.concatenate([gla_gate_w1[:, 0], gla_gate_w1[:, 1],
                          jnp.zeros((n_gla, D, V7X_LANES - 2 * rank), F32)], axis=-1)
    w2 = jnp.zeros((n_gla, 2, V7X_LANES, gla_gate_w2.shape[-1]), F32)
    w2 = w2.at[:, 0, :rank].set(gla_gate_w2[:, 0]).at[:, 1, rank:2 * rank].set(gla_gate_w2[:, 1])
    gla = dict(w_in=bf(gla_w_in), gate_w1=bf(w1), gate_w2=w2, gate_b=gla_gate_b[:, :, None, :],
               w_out=bf(gla_w_out))
    mw1, mw2 = bf(mlp_w1), bf(mlp_w2)

    nb_p, nb_s = c_prompt.shape[0], c_sample.shape[0]
    cs = jax.nn.silu(jnp.concatenate([c_prompt, c_sample], axis=0))
    rows = -(-cs.shape[0] // 16) * 16
    cs = jnp.pad(cs, ((0, rows - cs.shape[0]), (0, 0))).astype(BF16)
    mod = jnp.stack([_matmul(cs, ada_w, layer=i, bias=ada_b[i], bn=512, name="ada") for i in range(DEPTH)])
    mod_p, mod_s = mod[:, :nb_p], mod[:, nb_p:nb_p + nb_s]

    y_prompt = _trunk(x_prompt, mod_p, hy, ret, gla, norm_g, mw1, mw2, final_g)
    y_sample = _trunk(x_sample, mod_s, hy, ret, gla, norm_g, mw1, mw2, final_g)
    return (y_prompt, y_sample)
```

```python
import functools
import math

import jax
import jax.numpy as jnp
import numpy as np
from jax import lax
from jax.experimental import pallas as pl
from jax.experimental.pallas import tpu as pltpu

D_MODEL = 4096
DEPTH = 4
N_MIXERS = 3
EPS = 1e-6
HY_BANDS = 16
HY_FAST_DECAY = 0.3
HY_SLOW_DECAY = 1.5
HY_TARGET = 1e-2
HY_SHIFT = 0.05
RET_HEADS = 16
ROPE_BASE = 10000.0
GLA_HEADS = 4
GLA_GATE_RANK = 16
GLA_TAU = 16.0

F32 = jnp.float32
BF16 = jnp.bfloat16

V7X_VMEM_LIMIT_BYTES = 56 * 1024 * 1024
V7X_LANES = 128
MM_BLOCK_M = 1024
MM_BLOCK_N = 1024
MM_BLOCK_K = 4096
MM_DEEP_BLOCK_N = 512
MM_DEEP_BLOCK_K = 4096
NORM_BLOCK_ROWS = 512
RET_CHUNK = 256
RET_HEADS_PER_STEP = 16
GLA_CHUNK = 128
GLA_SUBBLOCK = 32
GLA_EXP_CLAMP = 80.0


def _cparams(*sem):
    return pltpu.CompilerParams(dimension_semantics=sem, vmem_limit_bytes=V7X_VMEM_LIMIT_BYTES)


def _mm_kernel(*refs, nk, has_bias, has_resid, act):
    it = iter(refs)
    x_ref, w_ref = next(it), next(it)
    b_ref = next(it) if has_bias else None
    r_ref = next(it) if has_resid else None
    g_ref = next(it) if has_resid else None
    o_ref = next(it)
    acc_ref = next(it) if nk > 1 else None

    part = jnp.dot(x_ref[...].astype(BF16), w_ref[...].astype(BF16), preferred_element_type=F32)

    def finish(acc):
        if has_bias:
            acc = acc + b_ref[...]
        if act == "relu2":
            acc = jnp.maximum(acc, 0.0)
            acc = acc * acc
        if has_resid:
            acc = r_ref[...] + g_ref[...] * acc
        o_ref[...] = acc.astype(o_ref.dtype)

    if nk == 1:
        finish(part)
    else:
        k = pl.program_id(2)

        @pl.when(k == 0)
        def _():
            acc_ref[...] = part

        @pl.when(jnp.logical_and(k > 0, k < nk - 1))
        def _():
            acc_ref[...] += part

        @pl.when(k == nk - 1)
        def _():
            finish(acc_ref[...] + part)


def _matmul(x, w, *, layer=None, bias=None, act=None, resid=None, gate=None, rows_per_gate=None,
            out_dtype=F32, bm=MM_BLOCK_M, bn=MM_BLOCK_N, bk=MM_BLOCK_K, name="matmul"):
    M, K = x.shape
    K2, N = w.shape[-2:]
    assert K == K2
    bm, bn, bk = min(bm, M, rows_per_gate or M), min(bn, N), min(bk, K)
    assert M % bm == 0 and N % bn == 0 and K % bk == 0
    nk = K // bk
    has_bias = bias is not None
    has_resid = resid is not None

    operands = [x, w]
    in_specs = [pl.BlockSpec((bm, bk), lambda i, j, k: (i, k)),
                pl.BlockSpec((bk, bn), lambda i, j, k: (k, j)) if layer is None else
                pl.BlockSpec((None, bk, bn), lambda i, j, k: (layer, k, j))]
    if has_bias:
        operands.append(bias.reshape(1, N).astype(F32))
        in_specs.append(pl.BlockSpec((1, bn), lambda i, j, k: (0, j)))
    if has_resid:
        assert rows_per_gate % bm == 0
        tiles_per_gate = rows_per_gate // bm
        operands.append(resid)
        in_specs.append(pl.BlockSpec((bm, bn), lambda i, j, k: (i, j)))
        operands.append(gate.reshape(gate.shape[0], 1, N).astype(F32))
        in_specs.append(pl.BlockSpec((None, 1, bn), lambda i, j, k: (i // tiles_per_gate, 0, j)))

    scratch = [pltpu.VMEM((bm, bn), F32)] if nk > 1 else []
    kern = functools.partial(_mm_kernel, nk=nk, has_bias=has_bias, has_resid=has_resid, act=act)
    return pl.pallas_call(
        kern,
        out_shape=jax.ShapeDtypeStruct((M, N), out_dtype),
        grid=(M // bm, N // bn, nk),
        in_specs=in_specs,
        out_specs=pl.BlockSpec((bm, bn), lambda i, j, k: (i, j)),
        scratch_shapes=scratch,
        compiler_params=_cparams("parallel", "parallel", "arbitrary"),
        name=name,
    )(*operands)


def _mm_deep_kernel(x_ref, w_ref, r_ref, g_ref, o_ref, acc_ref, *, nk, bn):
    k = pl.program_id(1)
    cols = pl.ds(pl.multiple_of(pl.program_id(2) * bn, bn), bn)
    part = jnp.dot(x_ref[...], w_ref[...], preferred_element_type=F32)

    @pl.when(k == 0)
    def _():
        acc_ref[:, cols] = part

    @pl.when(jnp.logical_and(k > 0, k < nk - 1))
    def _():
        acc_ref[:, cols] += part

    @pl.when(k == nk - 1)
    def _():
        o_ref[...] = r_ref[...] + g_ref[...] * (acc_ref[:, cols] + part)


def _matmul_deep(x, w, layer, resid, gate, rows_per_gate, *, bm=MM_BLOCK_M, bn=MM_DEEP_BLOCK_N,
                 bk=MM_DEEP_BLOCK_K, name):
    M, K = x.shape
    N = w.shape[-1]
    bm = min(bm, M, rows_per_gate)
    assert M % bm == 0 and N % bn == 0 and K % bk == 0 and rows_per_gate % bm == 0
    nk = K // bk
    assert nk >= 2
    tiles_per_gate = rows_per_gate // bm
    jcol = lambda k, j: jnp.where(k == nk - 1, j, 0)
    return pl.pallas_call(
        functools.partial(_mm_deep_kernel, nk=nk, bn=bn),
        out_shape=jax.ShapeDtypeStruct((M, N), F32),
        grid=(M // bm, nk, N // bn),
        in_specs=[pl.BlockSpec((bm, bk), lambda i, k, j: (i, k)),
                  pl.BlockSpec((None, bk, bn), lambda i, k, j: (layer, k, j)),
                  pl.BlockSpec((bm, bn), lambda i, k, j: (i, jcol(k, j))),
                  pl.BlockSpec((None, 1, bn), lambda i, k, j: (i // tiles_per_gate, 0, jcol(k, j)))],
        out_specs=pl.BlockSpec((bm, bn), lambda i, k, j: (i, jcol(k, j))),
        scratch_shapes=[pltpu.VMEM((bm, N), F32)],
        compiler_params=_cparams("parallel", "arbitrary", "arbitrary"),
        name=name,
    )(x, w, resid, gate.reshape(gate.shape[0], 1, N).astype(F32))


def _norm_kernel(x_ref, g_ref, *rest, modulate):
    if modulate:
        sc_ref, sh_ref, o_ref = rest
    else:
        (o_ref,) = rest
    x = x_ref[...]
    y = x * lax.rsqrt(jnp.mean(x * x, axis=-1, keepdims=True) + EPS) * g_ref[...]
    if modulate:
        y = y * (1.0 + sc_ref[...]) + sh_ref[...]
    o_ref[...] = y.astype(o_ref.dtype)


def _norm(x, g, scale=None, shift=None, *, out_dtype, name):
    B, L, D = x.shape
    rows = min(NORM_BLOCK_ROWS, L)
    assert L % rows == 0
    modulate = scale is not None
    operands = [x, g.reshape(1, D)]
    in_specs = [pl.BlockSpec((None, rows, D), lambda b, i: (b, i, 0)),
                pl.BlockSpec((1, D), lambda b, i: (0, 0))]
    if modulate:
        operands += [scale.reshape(B, 1, D), shift.reshape(B, 1, D)]
        in_specs += [pl.BlockSpec((None, 1, D), lambda b, i: (b, 0, 0))] * 2
    return pl.pallas_call(
        functools.partial(_norm_kernel, modulate=modulate),
        out_shape=jax.ShapeDtypeStruct((B, L, D), out_dtype),
        grid=(B, L // rows),
        in_specs=in_specs,
        out_specs=pl.BlockSpec((None, rows, D), lambda b, i: (b, i, 0)),
        compiler_params=_cparams("parallel", "parallel"),
        name=name,
    )(*operands)


def _silu(x):
    return x / (1.0 + jnp.exp(-x))


def _gated_head_norm(o, gate):
    o = o * lax.rsqrt(jnp.mean(o * o, axis=-1, keepdims=True) + EPS)
    return _silu(gate) * o


def _ret_kernel(lg_ref, q_ref, k_ref, v_ref, cos_ref, sin_ref, *rest, reverse, final, heads_per_step, hd):
    if final:
        op_ref, g_ref, o_ref, state_ref = rest
    else:
        o_ref, state_ref = rest
    C = q_ref.shape[0]
    half = hd // 2

    @pl.when(pl.program_id(2) == 0)
    def _():
        state_ref[...] = jnp.zeros_like(state_ref)

    cos, sin = cos_ref[...], sin_ref[...]
    row = lax.broadcasted_iota(jnp.int32, (C, C), 0)
    col = lax.broadcasted_iota(jnp.int32, (C, C), 1)
    ridx = lax.broadcasted_iota(jnp.int32, (C, 1), 0).astype(F32)
    if reverse:
        mask = col > row
        dist = (col - row).astype(F32)
        q_pow, k_pow = C - ridx, ridx
    else:
        mask = col <= row
        dist = (row - col).astype(F32)
        q_pow, k_pow = ridx + 1.0, C - 1.0 - ridx
    dist = jnp.where(mask, dist, 0.0)

    def rotate(x):
        x1, x2 = x[:, :half], x[:, half:]
        return jnp.concatenate([x1 * cos - x2 * sin, x1 * sin + x2 * cos], axis=1)

    for hh in range(heads_per_step):
        lg = lg_ref[pl.program_id(1) * heads_per_step + hh]
        sl = slice(hh * hd, (hh + 1) * hd)
        q = rotate(q_ref[:, sl].astype(F32))
        k = rotate(k_ref[:, sl].astype(F32)) * (hd ** -0.5)
        v = v_ref[:, sl].astype(BF16)
        intra = jnp.where(mask, jnp.exp(lg * dist), 0.0)
        s = lax.dot_general(q.astype(BF16), k.astype(BF16), (((1,), (1,)), ((), ())),
                            preferred_element_type=F32) * intra
        state = state_ref[hh]
        o = (jnp.dot(s.astype(BF16), v, preferred_element_type=F32)
             + jnp.dot((q * jnp.exp(lg * q_pow)).astype(BF16), state.astype(BF16),
                       preferred_element_type=F32))
        kd = (k * jnp.exp(lg * k_pow)).astype(BF16)
        state_ref[hh] = state * jnp.exp(lg * jnp.full((1, hd), C, F32)) + lax.dot_general(
            kd, v, (((0,), (0,)), ((), ())), preferred_element_type=F32)
        if final:
            o_ref[:, sl] = _gated_head_norm(op_ref[:, sl].astype(F32) + o,
                                            g_ref[:, sl].astype(F32)).astype(o_ref.dtype)
        else:
            o_ref[:, sl] = o.astype(o_ref.dtype)


def _ret_pass(z, log_gamma, cos, sin, o_prev, *, reverse):
    B, L, D4 = z.shape
    D = D4 // 4
    hd = D // RET_HEADS
    hps = RET_HEADS_PER_STEP
    W = hps * hd
    C = min(RET_CHUNK, L)
    n = L // C
    nblk = D // W
    final = o_prev is not None
    cidx = (lambda c: n - 1 - c) if reverse else (lambda c: c)

    def zspec(part):
        return pl.BlockSpec((None, C, W), lambda b, h, c, lg: (b, cidx(c), part * nblk + h))

    tspec = pl.BlockSpec((C, hd // 2), lambda b, h, c, lg: (cidx(c), 0))
    ospec = pl.BlockSpec((None, C, W), lambda b, h, c, lg: (b, cidx(c), h))
    operands = [z, z, z, cos, sin]
    in_specs = [zspec(0), zspec(1), zspec(2), tspec, tspec]
    if final:
        operands += [o_prev, z]
        in_specs += [ospec, zspec(3)]
    kern = functools.partial(_ret_kernel, reverse=reverse, final=final, heads_per_step=hps, hd=hd)
    return pl.pallas_call(
        kern,
        out_shape=jax.ShapeDtypeStruct((B, L, D), BF16),
        grid_spec=pltpu.PrefetchScalarGridSpec(
            num_scalar_prefetch=1,
            grid=(B, nblk, n),
            in_specs=in_specs,
            out_specs=ospec,
            scratch_shapes=[pltpu.VMEM((hps, hd, hd), F32)]),
        compiler_params=_cparams("parallel", "parallel", "arbitrary"),
        name="ret_bwd" if reverse else "ret_fwd",
    )(log_gamma, *operands)


def _pair_split_perm(d, heads):
    hd = d // heads
    idx = np.arange(d).reshape(heads, hd)
    return np.concatenate([idx[:, 0::2], idx[:, 1::2]], axis=1).reshape(-1)


def _retention_mixer(h, x, g1, p):
    B, L, D = x.shape
    hd = D // RET_HEADS
    z = _matmul(h.reshape(B * L, D), p["w_in"], layer=p["slot"], out_dtype=BF16,
                name="ret_in").reshape(B, L, 4 * D)
    pos = jnp.arange(L, dtype=F32)
    inv = 1.0 / (ROPE_BASE ** jnp.linspace(0.0, 1.0, hd // 2, dtype=F32))
    ang = pos[:, None] * inv[None, :]
    cos, sin = jnp.cos(ang), jnp.sin(ang)
    hidx = jnp.arange(RET_HEADS, dtype=F32)
    lg_fwd = jnp.log1p(-jnp.exp2(-5.0 - hidx))
    lg_bwd = jnp.log1p(-jnp.exp2(-5.0 - hidx[::-1]))
    o_fwd = _ret_pass(z, lg_fwd, cos, sin, None, reverse=False)
    y = _ret_pass(z, lg_bwd, cos, sin, o_fwd, reverse=True)
    out = _matmul(y.reshape(B * L, D), p["w_out"], layer=p["slot"], resid=x.reshape(B * L, D), gate=g1,
                  rows_per_gate=L, name="ret_out")
    return out.reshape(B, L, D)


def _gla_kernel(q_ref, k_ref, v_ref, t_ref, w2_ref, gb_ref, *rest, reverse, final, sub):
    if final:
        op_ref, r_ref, o_ref, state_ref, a_ref = rest
    else:
        o_ref, state_ref, a_ref = rest
    C, dk = q_ref.shape

    @pl.when(pl.program_id(2) == 0)
    def _():
        state_ref[...] = jnp.zeros_like(state_ref)

    zg = jnp.dot(t_ref[...].astype(BF16), w2_ref[...].astype(BF16), preferred_element_type=F32) + gb_ref[...]
    log_a = (jnp.minimum(zg, 0.0) - jnp.log1p(jnp.exp(-jnp.abs(zg)))) * (1.0 / GLA_TAU)

    row = lax.broadcasted_iota(jnp.int32, (C, C), 0)
    col = lax.broadcasted_iota(jnp.int32, (C, C), 1)
    tri = ((col >= row) if reverse else (col <= row)).astype(BF16)
    hi = log_a.astype(BF16)
    lo = (log_a - hi.astype(F32)).astype(BF16)
    b = (jnp.dot(tri, hi, preferred_element_type=F32) + jnp.dot(tri, lo, preferred_element_type=F32))
    b_tot = jnp.sum(log_a, axis=0, keepdims=True)

    q = q_ref[...].astype(F32) * (dk ** -0.5)
    k = k_ref[...].astype(F32)
    v = v_ref[...].astype(BF16)

    for blk in range(C // sub):
        rows = slice(blk * sub, (blk + 1) * sub)
        ref = b[blk * sub + sub // 2:blk * sub + sub // 2 + 1, :]
        qh = q[rows] * jnp.exp(jnp.minimum(b[rows] - ref, GLA_EXP_CLAMP))
        kh = k * jnp.exp(jnp.minimum(ref - b, GLA_EXP_CLAMP))
        s = lax.dot_general(qh.astype(BF16), kh.astype(BF16), (((1,), (1,)), ((), ())),
                            preferred_element_type=F32)
        r_i = lax.broadcasted_iota(jnp.int32, (sub, C), 0) + blk * sub
        c_i = lax.broadcasted_iota(jnp.int32, (sub, C), 1)
        keep = (c_i > r_i) if reverse else (c_i <= r_i)
        a_ref[rows, :] = jnp.where(keep, s, 0.0).astype(BF16)

    state = state_ref[...]
    o = (jnp.dot(a_ref[...], v, preferred_element_type=F32)
         + lax.dot_general((q * jnp.exp(b)).astype(BF16), state.astype(BF16), (((1,), (1,)), ((), ())),
                           preferred_element_type=F32))
    kd = (k * jnp.exp(b_tot - b)).astype(BF16)
    state_ref[...] = state * jnp.exp(b_tot) + lax.dot_general(
        v, kd, (((0,), (0,)), ((), ())), preferred_element_type=F32)
    if final:
        o_ref[...] = _gated_head_norm(op_ref[...].astype(F32) + o, r_ref[...].astype(F32)).astype(o_ref.dtype)
    else:
        o_ref[...] = o.astype(o_ref.dtype)


def _gla_pass(z, t, w2, gb, o_prev, *, direction):
    B, L, D3 = z.shape
    D = D3 // 3
    dk = D // 2 // GLA_HEADS
    dv = D // GLA_HEADS
    C = min(GLA_CHUNK, L)
    n = L // C
    reverse = direction == 1
    final = o_prev is not None
    cidx = (lambda c: n - 1 - c) if reverse else (lambda c: c)
    H = GLA_HEADS

    qspec = pl.BlockSpec((None, C, dk), lambda b, h, c: (b, cidx(c), h))
    kspec = pl.BlockSpec((None, C, dk), lambda b, h, c: (b, cidx(c), H + h))
    vspec = pl.BlockSpec((None, C, dv), lambda b, h, c: (b, cidx(c), H + h))
    rspec = pl.BlockSpec((None, C, dv), lambda b, h, c: (b, cidx(c), 2 * H + h))
    ospec = pl.BlockSpec((None, C, dv), lambda b, h, c: (b, cidx(c), h))
    tspec = pl.BlockSpec((None, C, t.shape[-1]), lambda b, h, c: (b, cidx(c), 0))
    w2spec = pl.BlockSpec((None, t.shape[-1], dk), lambda b, h, c: (direction, 0, h))
    gbspec = pl.BlockSpec((None, 1, dk), lambda b, h, c: (direction, 0, h))
    operands = [z, z, z, t, w2, gb]
    in_specs = [qspec, kspec, vspec, tspec, w2spec, gbspec]
    if final:
        operands += [o_prev, z]
        in_specs += [ospec, rspec]
    kern = functools.partial(_gla_kernel, reverse=reverse, final=final, sub=min(GLA_SUBBLOCK, C))
    return pl.pallas_call(
        kern,
        out_shape=jax.ShapeDtypeStruct((B, L, D), BF16),
        grid=(B, H, n),
        in_specs=in_specs,
        out_specs=ospec,
        scratch_shapes=[pltpu.VMEM((dv, dk), F32), pltpu.VMEM((C, C), BF16)],
        compiler_params=_cparams("parallel", "parallel", "arbitrary"),
        name="gla_bwd" if reverse else "gla_fwd",
    )(*operands)


def _gla_mixer(h, x, g1, p):
    B, L, D = x.shape
    h2 = h.reshape(B * L, D)
    z = _matmul(h2, p["w_in"], layer=p["slot"], out_dtype=BF16, name="gla_in").reshape(B, L, 3 * D)
    t = _matmul(h2, p["gate_w1"], layer=p["slot"], bn=V7X_LANES, name="gla_gate").reshape(B, L, V7X_LANES)
    o_fwd = _gla_pass(z, t, p["gate_w2"], p["gate_b"], None, direction=0)
    y = _gla_pass(z, t, p["gate_w2"], p["gate_b"], o_fwd, direction=1)
    out = _matmul(y.reshape(B * L, D), p["w_out"], layer=p["slot"], resid=x.reshape(B * L, D), gate=g1,
                  rows_per_gate=L, name="gla_out")
    return out.reshape(B, L, D)


CONV_HALO = 16


def _conv_gate_kernel(*refs, rows):
    groups = [refs[5 * g:5 * g + 5] for g in range(3)]
    x0_out, vv_out = refs[15:]
    i = pl.program_id(1)
    first = i == 0
    last = i == pl.num_programs(1) - 1
    h = CONV_HALO
    ct = groups[0][0].shape[1]
    row = lax.broadcasted_iota(jnp.int32, (rows, ct), 0)

    def put(out_ref, val):
        lanes = out_ref.shape[-1]
        for t in range(out_ref.shape[0]):
            out_ref[t] = val[:, t * lanes:(t + 1) * lanes].astype(out_ref.dtype)

    def conv(cur_ref, prev_ref, next_ref, w_ref, b_ref):
        cur = cur_ref[...].astype(F32)
        before = jnp.where(first, 0.0, prev_ref[h - 1:h, :].astype(F32))
        after = jnp.where(last, 0.0, next_ref[0:1, :].astype(F32))
        prev = jnp.where(row == 0, before, pltpu.roll(cur, 1, axis=0))
        nxt = jnp.where(row == rows - 1, after, pltpu.roll(cur, rows - 1, axis=0))
        w = w_ref[...]
        return prev * w[0:1] + cur * w[1:2] + nxt * w[2:3] + b_ref[...]

    put(x0_out, conv(*groups[0]))
    x1 = conv(*groups[1])
    put(vv_out, conv(*groups[2]) * x1)


def _conv_gate(z, conv_w, conv_b):
    B, L, D3 = z.shape
    D = D3 // 3
    rows, ct = min(512, L), 512
    nct = D // ct
    h = CONV_HALO
    nhalo = L // h
    operands, in_specs = [], []
    for g in range(3):
        operands += [z, z, z, conv_w, conv_b.reshape(1, D3)]
        in_specs += [
            pl.BlockSpec((None, rows, ct), lambda b, i, j, g=g: (b, i, g * nct + j)),
            pl.BlockSpec((None, h, ct), lambda b, i, j, g=g: (b, jnp.maximum(i * (rows // h) - 1, 0), g * nct + j)),
            pl.BlockSpec((None, h, ct), lambda b, i, j, g=g: (b, jnp.minimum((i + 1) * (rows // h), nhalo - 1), g * nct + j)),
            pl.BlockSpec((3, ct), lambda b, i, j, g=g: (0, g * nct + j)),
            pl.BlockSpec((1, ct), lambda b, i, j, g=g: (0, g * nct + j)),
        ]
    tiles = ct // FFT_LANES
    ospec = pl.BlockSpec((None, tiles, rows, FFT_LANES), lambda b, i, j: (b, j, i, 0))
    oshape = jax.ShapeDtypeStruct((B, D // FFT_LANES, L, FFT_LANES), BF16)
    return pl.pallas_call(
        functools.partial(_conv_gate_kernel, rows=rows),
        out_shape=(oshape, oshape),
        grid=(B, L // rows, nct),
        in_specs=in_specs,
        out_specs=(ospec, ospec),
        compiler_params=_cparams("parallel", "parallel", "parallel"),
        name="hy_conv_gate",
    )(*operands)


FFT_GROUP = 16
FFT_K1_GROUP = 32
FFT_LANES = 256


def _stage_a(x_parts, g_ref, z_ref, stage_ref, step, *, n1, ng, mirror_second=False):
    xt = [jnp.swapaxes(x, 0, 1) for x in x_parts]
    for j in range(FFT_GROUP):
        x = jnp.concatenate([xt[0][j], xt[1][FFT_GROUP - 1 - j if mirror_second else j]], axis=0)
        stage_ref[j] = jnp.dot(g_ref[j], x.astype(BF16), preferred_element_type=F32)
    a = jnp.swapaxes(stage_ref[...], 0, 1)
    z_ref[:, step] = a[:n1].astype(z_ref.dtype)
    z_ref[:, ng + step] = a[n1:].astype(z_ref.dtype)


def _fft_conv_kernel(x_ref, x0_ref, g_ref, f_ref, fi_ref, kf_ref, gi_ref, skip_ref, o_ref, z_ref, sa_ref, sc_ref,
                     *, sa, sb, n1, n2, h1):
    step = pl.program_id(2)
    ng = n2 // FFT_GROUP

    @pl.when(step < sa)
    def _():
        _stage_a([x_ref[0].astype(F32), x_ref[1].astype(F32)], g_ref, z_ref, sa_ref, step, n1=n1, ng=ng)

    @pl.when(jnp.logical_and(step >= sa, step < sa + sb))
    def _():
        for j in range(FFT_K1_GROUP):
            k1 = (step - sa) * FFT_K1_GROUP + j
            zk = z_ref[k1].reshape(2 * n2, z_ref.shape[-1])
            y = jnp.dot(f_ref[...], zk, preferred_element_type=F32)
            yr, yi = y[:n2], y[n2:]
            kr, ki = kf_ref[j, :n2, :], kf_ref[j, n2:, :]
            p = jnp.concatenate([yr * kr - yi * ki, yr * ki + yi * kr], axis=0).astype(BF16)
            back = jnp.dot(fi_ref[...], p, preferred_element_type=F32)
            z_ref[k1] = back.astype(z_ref.dtype).reshape(z_ref.shape[1:])

    @pl.when(step >= sa + sb)
    def _():
        g = step - sa - sb
        zr = jnp.swapaxes(z_ref[:, g].astype(F32), 0, 1)
        zi = jnp.swapaxes(z_ref[:, ng + g].astype(F32), 0, 1)
        for j in range(FFT_GROUP):
            bz = jnp.concatenate([zr[j], zi[j]], axis=0).astype(BF16)
            sc_ref[j] = jnp.dot(gi_ref[j], bz, preferred_element_type=F32)
        y = jnp.swapaxes(sc_ref[...], 0, 1).reshape(x_ref.shape)
        o_ref[...] = ((y + skip_ref[...] * x_ref[...].astype(F32)) * x0_ref[...].astype(F32)).astype(o_ref.dtype)


def _dft_tables(L):
    N = 2 * L
    n1 = n2 = int(round(math.sqrt(N)))
    assert n1 * n2 == N and n2 % FFT_GROUP == 0 and n1 % FFT_K1_GROUP == 0
    h1 = n1 // 2
    k1 = jnp.arange(n1, dtype=jnp.int32)
    m = (jnp.arange(n2, dtype=jnp.int32)[:, None, None] * k1[None, :, None]
         + n2 * k1[None, :, None] * jnp.arange(h1, dtype=jnp.int32)[None, None, :]) % N
    ang = (2.0 * math.pi / N) * m.astype(F32)
    c, s = jnp.cos(ang), jnp.sin(ang)
    block = lambda re, im: jnp.concatenate([jnp.concatenate([re, -im], -1), jnp.concatenate([im, re], -1)], -2)
    g = block(c, -s).astype(BF16)
    ct_, st_ = jnp.swapaxes(c, 1, 2) / N, jnp.swapaxes(s, 1, 2) / N
    gi = block(ct_, st_).astype(BF16)
    idx = jnp.arange(n2, dtype=jnp.int32)
    ang2 = (2.0 * math.pi / n2) * ((idx[:, None] * idx[None, :]) % n2).astype(F32)
    c2, s2 = jnp.cos(ang2), jnp.sin(ang2)
    return dict(n1=n1, n2=n2, h1=h1, g=g, gi=gi, f=block(c2, -s2).astype(BF16), fi=block(c2, s2).astype(BF16))


def _filter_spectrum_kernel(hf_ref, hb_ref, g_ref, f_ref, sum_ref, o_ref, z_ref, sa_ref, *, sa, n1, n2):
    step = pl.program_id(1)
    ng = n2 // FFT_GROUP

    @pl.when(step < sa)
    def _():
        _stage_a([hf_ref[...], hb_ref[...]], g_ref, z_ref, sa_ref, step, n1=n1, ng=ng, mirror_second=True)

    @pl.when(step >= sa)
    def _():
        scale = 1.0 / (sum_ref[...] + EPS)
        for j in range(FFT_K1_GROUP):
            k1 = (step - sa) * FFT_K1_GROUP + j
            zk = z_ref[k1].reshape(2 * n2, z_ref.shape[-1])
            o_ref[j] = (jnp.dot(f_ref[...], zk, preferred_element_type=F32) * scale).astype(o_ref.dtype)


def _filter_tables(tab):
    n1, n2, h1 = tab["n1"], tab["n2"], tab["h1"]
    N = n1 * n2
    k1 = jnp.arange(n1, dtype=jnp.int32)[None, :, None]
    i2 = jnp.arange(n2, dtype=jnp.int32)[:, None, None]
    a = jnp.arange(h1, dtype=jnp.int32)[None, None, :]
    rows = jnp.concatenate([a, n1 - 1 - a], axis=-1)
    m = (i2 * k1 + n2 * k1 * rows) % N
    ang = (2.0 * math.pi / N) * m.astype(F32)
    col = jnp.arange(2 * h1, dtype=jnp.int32)[None, None, :]
    live = jnp.logical_not(jnp.logical_and(i2 == 0, col == 2 * h1 - 1))
    re = jnp.where(live, jnp.cos(ang), 0.0)
    im = jnp.where(live, -jnp.sin(ang), 0.0)
    return jnp.concatenate([re, im], axis=1).astype(BF16)


def _filter_spectrum(hf, hb, abs_sum, tab):
    L, D = hf.shape
    n1, n2, h1 = tab["n1"], tab["n2"], tab["h1"]
    ct = FFT_LANES
    sa = n2 // FFT_GROUP
    sb = n1 // FFT_K1_GROUP
    kern = functools.partial(_filter_spectrum_kernel, sa=sa, n1=n1, n2=n2)
    return pl.pallas_call(
        kern,
        out_shape=jax.ShapeDtypeStruct((n1, 2 * n2, D), BF16),
        grid=(D // ct, sa + sb),
        in_specs=[
            pl.BlockSpec((h1, FFT_GROUP, ct), lambda t, s: (0, jnp.minimum(s, sa - 1), t)),
            pl.BlockSpec((h1, FFT_GROUP, ct), lambda t, s: (0, sa - 1 - jnp.minimum(s, sa - 1), t)),
            pl.BlockSpec((FFT_GROUP, 2 * n1, 2 * h1), lambda t, s: (jnp.minimum(s, sa - 1), 0, 0)),
            pl.BlockSpec((2 * n2, 2 * n2), lambda t, s: (0, 0)),
            pl.BlockSpec((1, ct), lambda t, s: (0, t)),
        ],
        out_specs=pl.BlockSpec((FFT_K1_GROUP, 2 * n2, ct), lambda t, s: (jnp.maximum(s - sa, 0), 0, t)),
        scratch_shapes=[pltpu.VMEM((n1, 2 * n2 // FFT_GROUP, FFT_GROUP, ct), BF16),
                        pltpu.VMEM((FFT_GROUP, 2 * n1, ct), F32)],
        compiler_params=_cparams("parallel", "arbitrary"),
        name="hy_filter_spectrum",
    )(hf.reshape(h1, n2, D), hb.reshape(h1, n2, D), tab["gf"], tab["f"], abs_sum)


def _fft_conv(vv, x0, kf, skip, tab):
    B, ntile, L, ct = vv.shape
    D = ntile * ct
    n1, n2, h1 = tab["n1"], tab["n2"], tab["h1"]
    assert B % 2 == 0 and L == h1 * n2 and ct == FFT_LANES
    sa = n2 // FFT_GROUP
    sb = n1 // FFT_K1_GROUP
    sc = sa
    shape5 = (B // 2, 2, h1, n2, D)
    shape6 = (B // 2, 2, ntile, h1, n2, ct)
    a_idx = lambda s: jnp.where(s < sa, s, jnp.where(s >= sa + sb, s - sa - sb, sa - 1))
    c_idx = lambda s: jnp.clip(s - sa - sb, 0, sc - 1)
    blk = (None, 2, h1, FFT_GROUP, ct)
    blk6 = (None, 2, None, h1, FFT_GROUP, ct)
    in_specs = [
        pl.BlockSpec(blk6, lambda p, t, s: (p, 0, t, 0, a_idx(s), 0)),
        pl.BlockSpec(blk6, lambda p, t, s: (p, 0, t, 0, c_idx(s), 0)),
        pl.BlockSpec((FFT_GROUP, 2 * n1, 2 * h1), lambda p, t, s: (jnp.minimum(s, sa - 1), 0, 0)),
        pl.BlockSpec((2 * n2, 2 * n2), lambda p, t, s: (0, 0)),
        pl.BlockSpec((2 * n2, 2 * n2), lambda p, t, s: (0, 0)),
        pl.BlockSpec((FFT_K1_GROUP, 2 * n2, ct), lambda p, t, s: (jnp.clip(s - sa, 0, sb - 1), 0, t)),
        pl.BlockSpec((FFT_GROUP, 2 * h1, 2 * n1), lambda p, t, s: (c_idx(s), 0, 0)),
        pl.BlockSpec((1, ct), lambda p, t, s: (0, t)),
    ]
    kern = functools.partial(_fft_conv_kernel, sa=sa, sb=sb, n1=n1, n2=n2, h1=h1)
    y = pl.pallas_call(
        kern,
        out_shape=jax.ShapeDtypeStruct(shape5, BF16),
        grid=(B // 2, D // ct, sa + sb + sc),
        in_specs=in_specs,
        out_specs=pl.BlockSpec(blk, lambda p, t, s: (p, 0, 0, c_idx(s), t)),
        scratch_shapes=[pltpu.VMEM((n1, 2 * n2 // FFT_GROUP, FFT_GROUP, ct), BF16),
                        pltpu.VMEM((FFT_GROUP, 2 * n1, ct), F32),
                        pltpu.VMEM((FFT_GROUP, 2 * h1, ct), F32)],
        compiler_params=_cparams("parallel", "parallel", "arbitrary"),
        name="hy_fft_conv",
    )(vv.reshape(shape6), x0.reshape(shape6), tab["g"], tab["f"], tab["fi"], kf, tab["gi"],
      skip.reshape(1, D))
    return y.reshape(B, L, D)


def _filter_taps_kernel(feat_ref, t_ref, wf_ref, wb_ref, dl_ref, hf_ref, hb_ref, sum_ref, *, rows, total_rows):
    i = pl.program_id(1)
    window = jnp.exp(-t_ref[...] * dl_ref[...]) + HY_SHIFT
    feat = feat_ref[...].astype(BF16)
    hf = jnp.dot(feat, wf_ref[...].astype(BF16), preferred_element_type=F32) * window
    hb = jnp.dot(feat, wb_ref[...].astype(BF16), preferred_element_type=F32) * window
    hf_ref[...] = hf
    hb_ref[...] = hb
    row = lax.broadcasted_iota(jnp.int32, hb.shape, 0) + i * rows
    part = (jnp.sum(jnp.abs(hf), axis=0, keepdims=True)
            + jnp.sum(jnp.where(row < total_rows - 1, jnp.abs(hb), 0.0), axis=0, keepdims=True))

    @pl.when(i == 0)
    def _():
        sum_ref[...] = part

    @pl.when(i > 0)
    def _():
        sum_ref[...] += part


def _filter_taps(L, w1, b1, w2, b2, w3, b3, w_out, freq):
    hp = lax.Precision.HIGHEST
    D = w_out.shape[-1] // 2
    t = jnp.linspace(0.0, 1.0, L, dtype=F32)[:, None]
    w = (2.0 * math.pi / L) * jnp.arange(L, dtype=F32)[:, None]
    bands = jnp.linspace(1e-4, HY_BANDS - 1, HY_BANDS, dtype=F32)[None, :]
    z = jnp.concatenate([t, jnp.cos(bands * w), -jnp.sin(bands * w)], axis=-1)
    h = jnp.sin(freq * (jnp.dot(z, w1, precision=hp) + b1))
    h = jnp.sin(freq * (jnp.dot(h, w2, precision=hp) + b2))
    h = jnp.sin(freq * (jnp.dot(h, w3, precision=hp) + b3))
    max_decay = math.log(HY_TARGET) / HY_FAST_DECAY
    min_decay = math.log(HY_TARGET) / HY_SLOW_DECAY
    deltas = jnp.abs(jnp.linspace(min_decay, max_decay, D, dtype=F32))[None, :]
    rows, ct = min(512, L), 512
    width = h.shape[-1]
    nct = D // ct
    tap_spec = pl.BlockSpec((rows, ct), lambda j, i: (i, j))
    return pl.pallas_call(
        functools.partial(_filter_taps_kernel, rows=rows, total_rows=L),
        out_shape=(jax.ShapeDtypeStruct((L, D), F32), jax.ShapeDtypeStruct((L, D), F32),
                   jax.ShapeDtypeStruct((1, D), F32)),
        grid=(nct, L // rows),
        in_specs=[pl.BlockSpec((rows, width), lambda j, i: (i, 0)),
                  pl.BlockSpec((rows, 1), lambda j, i: (i, 0)),
                  pl.BlockSpec((width, ct), lambda j, i: (0, j)),
                  pl.BlockSpec((width, ct), lambda j, i: (0, nct + j)),
                  pl.BlockSpec((1, ct), lambda j, i: (0, j))],
        out_specs=(tap_spec, tap_spec, pl.BlockSpec((1, ct), lambda j, i: (0, j))),
        compiler_params=_cparams("parallel", "arbitrary"),
        name="hy_filter_taps",
    )(h, t, w_out, w_out, deltas)


def _hyena_mixer(h, x, g1, p, tab):
    B, L, D = x.shape
    z = _matmul(h.reshape(B * L, D), p["w_in"], layer=p["slot"], bias=p["b_in"], out_dtype=BF16,
                name="hy_in").reshape(B, L, 3 * D)
    x0, vv = _conv_gate(z, p["conv_w"], p["conv_b"])
    hf, hb, abs_sum = _filter_taps(L, p["f_w1"], p["f_b1"], p["f_w2"], p["f_b2"], p["f_w3"], p["f_b3"],
                                   p["f_wout"], p["freq"])
    y = _fft_conv(vv, x0, _filter_spectrum(hf, hb, abs_sum, tab), p["skip"], tab)
    out = _matmul(y.reshape(B * L, D), p["w_out"], layer=p["slot"], bias=p["b_out"],
                  resid=x.reshape(B * L, D), gate=g1, rows_per_gate=L, name="hy_out")
    return out.reshape(B, L, D)


STACKED_WEIGHTS = ("w_in", "w_out", "gate_w1")


def _layer_params(params, slot):
    p = {n: (v if n in STACKED_WEIGHTS else v[slot]) for n, v in params.items()}
    p["slot"] = slot
    return p


def _trunk(x, mod, hy, ret, gla, norm_g, mlp_w1, mlp_w2, final_g):
    B, L, D = x.shape
    tab = _dft_tables(L)
    tab["gf"] = _filter_tables(tab)
    for i in range(DEPTH):
        sh1, sc1, g1, sh2, sc2, g2 = jnp.split(mod[i], 6, axis=-1)
        h = _norm(x, norm_g[i, 0], sc1, sh1, out_dtype=BF16, name="norm_mix")
        kind, slot = i % N_MIXERS, i // N_MIXERS
        if kind == 0:
            x = _hyena_mixer(h, x, g1, _layer_params(hy, slot), tab)
        elif kind == 1:
            x = _retention_mixer(h, x, g1, _layer_params(ret, slot))
        else:
            x = _gla_mixer(h, x, g1, _layer_params(gla, slot))
        h = _norm(x, norm_g[i, 1], sc2, sh2, out_dtype=BF16, name="norm_mlp")
        a = _matmul(h.reshape(B * L, D), mlp_w1, layer=i, act="relu2", out_dtype=BF16, name="mlp_up")
        x = _matmul_deep(a, mlp_w2, i, x.reshape(B * L, D), g2, L, name="mlp_down").reshape(B, L, D)
    return _norm(x, final_g, out_dtype=F32, name="norm_final")


def kernel(x_prompt, x_sample, c_prompt, c_sample, hy_w_in, hy_b_in, hy_conv_w, hy_conv_b, hy_f_w1, hy_f_b1, hy_f_w2, hy_f_b2, hy_f_w3, hy_f_b3, hy_f_wout, hy_freq, hy_skip, hy_w_out, hy_b_out, ret_w_in, ret_w_out, gla_w_in, gla_gate_w1, gla_gate_w2, gla_gate_b, gla_w_out, norm_g, ada_w, ada_b, mlp_w1, mlp_w2, final_g):
    D = x_prompt.shape[-1]
    bf = lambda w: w.astype(BF16)
    hy = dict(w_in=bf(hy_w_in), b_in=hy_b_in, conv_w=hy_conv_w, conv_b=hy_conv_b,
              f_w1=hy_f_w1, f_b1=hy_f_b1, f_w2=hy_f_w2, f_b2=hy_f_b2, f_w3=hy_f_w3, f_b3=hy_f_b3,
              f_wout=hy_f_wout, freq=hy_freq, skip=hy_skip, w_out=bf(hy_w_out), b_out=hy_b_out)

    perm = _pair_split_perm(D, RET_HEADS)
    cols = np.concatenate([perm, D + perm, np.arange(2 * D, 4 * D)])
    ret = dict(w_in=bf(ret_w_in[:, :, cols]), w_out=bf(ret_w_out))

    n_gla, _, _, rank = gla_gate_w1.shape
    w1 = jnp.concatenate([gla_gate_w1[:, 0], gla_gate_w1[:, 1],
                          jnp.zeros((n_gla, D, V7X_LANES - 2 * rank), F32)], axis=-1)
    w2 = jnp.zeros((n_gla, 2, V7X_LANES, gla_gate_w2.shape[-1]), F32)
    w2 = w2.at[:, 0, :rank].set(gla_gate_w2[:, 0]).at[:, 1, rank:2 * rank].set(gla_gate_w2[:, 1])
    gla = dict(w_in=bf(gla_w_in), gate_w1=bf(w1), gate_w2=w2, gate_b=gla_gate_b[:, :, None, :],
               w_out=bf(gla_w_out))
    mw1, mw2 = bf(mlp_w1), bf(mlp_w2)

    nb_p, nb_s = c_prompt.shape[0], c_sample.shape[0]
    cs = jax.nn.silu(jnp.concatenate([c_prompt, c_sample], axis=0))
    rows = -(-cs.shape[0] // 16) * 16
    cs = jnp.pad(cs, ((0, rows - cs.shape[0]), (0, 0))).astype(BF16)
    mod = jnp.stack([_matmul(cs, ada_w, layer=i, bias=ada_b[i], bn=512, name="ada") for i in range(DEPTH)])
    mod_p, mod_s = mod[:, :nb_p], mod[:, nb_p:nb_p + nb_s]

    y_prompt = _trunk(x_prompt, mod_p, hy, ret, gla, norm_g, mw1, mw2, final_g)
    y_sample = _trunk(x_sample, mod_s, hy, ret, gla, norm_g, mw1, mw2, final_g)
    return (y_prompt, y_sample)
```
